```python
import math, functools
import jax, jax.numpy as jnp
from jax import lax
import numpy as np

D_MODEL = 1024
BATCH = 2
SEQ = 8192
DEPTH = 2
DEC_BATCH = 128
DEC_SEQ = 4
PAST_LEN = 2048
PAGE_SIZE = 128

N_BRANCH = 4
W_MIX = D_MODEL // 4
W_SSM = W_MIX
SSM_GROUP = 16
SSM_GROUPS = W_SSM // SSM_GROUP
SSM_P = 64
W_CONV = W_MIX
CONV_B_W = 3
GDN_H = 4
GDN_DK = W_MIX // GDN_H
GDN_DV = W_MIX // GDN_H
GDN_CONV = 4
GDN_CHUNK = 64
W_GDN_QKV = 2 * GDN_H * GDN_DK + GDN_H * GDN_DV
SB_H = 4
SB_DH = W_MIX // SB_H
SB_BLOCK = 128
D_FF = ((8 * D_MODEL + 3 * 256 - 1) // (3 * 256)) * 256
N_IN = W_SSM + 3 * W_CONV + W_GDN_QKV + GDN_H * GDN_DV + 2 * GDN_H + 3 * SB_H * SB_DH
ALPHA_DN = (2 * DEPTH) ** 0.25
BETA_DN = (8 * DEPTH) ** -0.25
LN_EPS = 1e-5
RMS_EPS = 1e-6

kernel_name = 'hybrid_gated_branch_decoder_step'


def layer_norm(x, g=None, b=None):
    xf = x.astype(jnp.float32)
    mu = jnp.mean(xf, axis=-1, keepdims=True)
    var = jnp.mean(jnp.square(xf - mu), axis=-1, keepdims=True)
    y = (xf - mu) * lax.rsqrt(var + LN_EPS)
    if g is not None:
        y = y * g.astype(jnp.float32) + b.astype(jnp.float32)
    return y.astype(x.dtype)


def l2norm(x):
    return x * lax.rsqrt(jnp.sum(jnp.square(x), axis=-1, keepdims=True) + RMS_EPS)


def causal_conv(u, w, buf):
    k_w = w.shape[0]
    L = u.shape[1]
    xc = jnp.concatenate([buf.astype(u.dtype), u], axis=1)
    out = xc[:, 0:L] * w[0]
    for i in range(1, k_w):
        out = out + xc[:, i:i + L] * w[i]
    return out, xc[:, L:]


def cmul(ar, ai, br, bi):
    return ar * br - ai * bi, ar * bi + ai * br


def s5_ssm(u, a_re, a_im, log_dt, b_re, b_im, c_re, c_im, d, h0_re, h0_im):
    f32 = jnp.float32
    bsz, L, _ = u.shape
    uf = u.astype(f32).reshape(bsz, L, SSM_GROUPS, SSM_GROUP)
    a_re = a_re.astype(f32)
    a_im = a_im.astype(f32)
    dt = jnp.exp(log_dt.astype(f32))[:, None]
    mag = jnp.exp(dt * a_re)
    ab_re = mag * jnp.cos(dt * a_im)
    ab_im = mag * jnp.sin(dt * a_im)
    den = a_re * a_re + a_im * a_im
    f_re = ((ab_re - 1.0) * a_re + ab_im * a_im) / den
    f_im = (ab_im * a_re - (ab_re - 1.0) * a_im) / den
    bb_re, bb_im = cmul(f_re[..., None], f_im[..., None], b_re.astype(f32), b_im.astype(f32))
    bu_re = jnp.einsum('blgh,gph->blgp', uf, bb_re)
    bu_im = jnp.einsum('blgh,gph->blgp', uf, bb_im)
    a_b_re = jnp.broadcast_to(ab_re, bu_re.shape)
    a_b_im = jnp.broadcast_to(ab_im, bu_re.shape)

    def combine(e1, e2):
        a1r, a1i, b1r, b1i = e1
        a2r, a2i, b2r, b2i = e2
        ar, ai = cmul(a2r, a2i, a1r, a1i)
        br, bi = cmul(a2r, a2i, b1r, b1i)
        return ar, ai, br + b2r, bi + b2i

    p_re, p_im, s_re, s_im = lax.associative_scan(combine, (a_b_re, a_b_im, bu_re, bu_im), axis=1)
    i_re, i_im = cmul(p_re, p_im, h0_re.astype(f32)[:, None], h0_im.astype(f32)[:, None])
    h_re = i_re + s_re
    h_im = i_im + s_im
    y = (jnp.einsum('blgp,ghp->blgh', h_re, c_re.astype(f32))
         - jnp.einsum('blgp,ghp->blgh', h_im, c_im.astype(f32))
         + d.astype(f32).reshape(SSM_GROUPS, SSM_GROUP) * uf)
    return y.reshape(bsz, L, W_SSM).astype(u.dtype), h_re[:, -1], h_im[:, -1]


def gated_delta_chunked(q, k, v, g, beta, s0):
    f32 = jnp.float32
    bsz, L, nh, dk = q.shape
    dv = v.shape[-1]
    cs = min(GDN_CHUNK, L)
    n_c = -(-L // cs)
    pad = n_c * cs - L

    def blocks(t):
        t = jnp.pad(t.astype(f32), [(0, 0), (0, pad)] + [(0, 0)] * (t.ndim - 2))
        t = t.reshape((bsz, n_c, cs) + t.shape[2:])
        return jnp.moveaxis(jnp.moveaxis(t, 1, 0), 2, 3)

    qb, kb, vb, gb, bb = blocks(q), blocks(k), blocks(v), blocks(g), blocks(beta)
    gc = jnp.cumsum(gb, axis=-1)
    idx = jnp.arange(cs)
    strict = idx[:, None] > idx[None, :]
    incl = idx[:, None] >= idx[None, :]
    diff = gc[..., :, None] - gc[..., None, :]
    dec_strict = jnp.where(strict, jnp.exp(jnp.where(strict, diff, 0.0)), 0.0)
    dec_incl = jnp.where(incl, jnp.exp(jnp.where(incl, diff, 0.0)), 0.0)
    m = bb[..., :, None] * jnp.einsum('nbhid,nbhjd->nbhij', kb, kb) * dec_strict
    egc = jnp.exp(gc)
    rhs = jnp.concatenate([bb[..., None] * vb, (bb * egc)[..., None] * kb], axis=-1)
    sol = lax.linalg.triangular_solve(m + jnp.eye(cs, dtype=f32), rhs, left_side=True, lower=True, unit_diagonal=True)
    u_v, w = sol[..., :dv], sol[..., dv:]
    qk = jnp.einsum('nbhid,nbhjd->nbhij', qb, kb) * dec_incl
    q_dec = qb * egc[..., None]
    g_last = gc[..., -1]
    k_dec = kb * jnp.exp(g_last[..., None] - gc)[..., None]

    def step(S, xs):
        u_v_c, w_c, q_c, qk_c, k_c, gl_c = xs
        u = u_v_c - jnp.einsum('bhik,bhkv->bhiv', w_c, S)
        o = jnp.einsum('bhik,bhkv->bhiv', q_c, S) + jnp.einsum('bhij,bhjv->bhiv', qk_c, u)
        S = jnp.exp(gl_c)[..., None, None] * S + jnp.einsum('bhik,bhiv->bhkv', k_c, u)
        return S, o

    s_fin, o = lax.scan(step, s0.astype(f32), (u_v, w, q_dec, qk, k_dec, g_last))
    o = jnp.moveaxis(jnp.moveaxis(o, 3, 2), 0, 1).reshape(bsz, n_c * cs, nh, dv)[:, :L]
    return o, s_fin


def stick_breaking(q, k, v, q_pos, k_pos):
    f32 = jnp.float32
    z = jnp.einsum('bqhd,bkhd->bhqk', q.astype(f32), k.astype(f32)) * (q.shape[-1] ** -0.5)
    causal = k_pos[None, :] < q_pos[:, None]
    log_neg = jnp.where(causal, jax.nn.log_sigmoid(-z), 0.0)
    csum = jnp.cumsum(log_neg, axis=-1)
    log_w = jax.nn.log_sigmoid(z) + csum[..., -1:] - csum
    wts = jnp.where(causal, jnp.exp(log_w), 0.0)
    return jnp.einsum('bhqk,bkhd->bqhd', wts, v.astype(f32)).astype(v.dtype)


def sb_prompt(q, k, v):
    bsz, L, nh, dh = q.shape
    nb = L // SB_BLOCK
    qb = jnp.moveaxis(q.reshape(bsz, nb, SB_BLOCK, nh, dh), 1, 0)
    k_pos = jnp.arange(L)

    def blk(args):
        qi, i = args
        q_pos = i * SB_BLOCK + jnp.arange(SB_BLOCK)
        return stick_breaking(qi, k, v, q_pos, k_pos)

    o = lax.map(blk, (qb, jnp.arange(nb)))
    return jnp.moveaxis(o, 0, 1).reshape(bsz, L, nh, dh)


def sb_sample(q, k, v, k_past, v_past):
    past = k_past.shape[1]
    t_new = q.shape[1]
    kk = jnp.concatenate([k_past.astype(k.dtype), k], axis=1)
    vv = jnp.concatenate([v_past.astype(v.dtype), v], axis=1)
    q_pos = past + jnp.arange(t_new)
    k_pos = jnp.arange(past + t_new)
    return stick_breaking(q, kk, vv, q_pos, k_pos)


def decoder_layer(x, c, p, st, attend):
    f32 = jnp.float32
    h0_re, h0_im, convb_buf, s0, convd_buf = st
    bsz, L, _ = x.shape
    mod = (jax.nn.silu(c) @ p['w_ada'] + p['b_ada'])[:, None, :]
    sh1, sc1, g1, sh2, sc2, g2 = jnp.split(mod, 6, axis=-1)
    h = layer_norm(x) * (1.0 + sc1) + sh1

    proj = h @ p['w_in']
    o = 0
    u_ssm = proj[..., o:o + W_SSM]; o += W_SSM
    b_g = proj[..., o:o + W_CONV]; o += W_CONV
    c_g = proj[..., o:o + W_CONV]; o += W_CONV
    x_c = proj[..., o:o + W_CONV]; o += W_CONV
    qkv_g = proj[..., o:o + W_GDN_QKV]; o += W_GDN_QKV
    z_g = proj[..., o:o + GDN_H * GDN_DV]; o += GDN_H * GDN_DV
    beta_logit = proj[..., o:o + GDN_H]; o += GDN_H
    a_logit = proj[..., o:o + GDN_H]; o += GDN_H
    q_sb = proj[..., o:o + SB_H * SB_DH].reshape(bsz, L, SB_H, SB_DH); o += SB_H * SB_DH
    k_sb = proj[..., o:o + SB_H * SB_DH].reshape(bsz, L, SB_H, SB_DH); o += SB_H * SB_DH
    v_sb = proj[..., o:o + SB_H * SB_DH].reshape(bsz, L, SB_H, SB_DH)

    y_s, h_re, h_im = s5_ssm(u_ssm, p['ssm_a_re'], p['ssm_a_im'], p['ssm_log_dt'], p['ssm_b_re'], p['ssm_b_im'],
                             p['ssm_c_re'], p['ssm_c_im'], p['ssm_d'], h0_re, h0_im)
    y_s = jax.nn.gelu(y_s)
    y_a = y_s * jax.nn.sigmoid(y_s @ p['w_glu'])

    conv_out, convb_new = causal_conv(c_g * x_c, p['conv_b_w'], convb_buf)
    y_b = b_g * conv_out

    qkv_c, convd_new = causal_conv(qkv_g, p['delta_conv_w'], convd_buf)
    qkv_c = jax.nn.silu(qkv_c.astype(f32))
    qd = l2norm(qkv_c[..., :GDN_H * GDN_DK].reshape(bsz, L, GDN_H, GDN_DK)) * (GDN_DK ** -0.5)
    kd = l2norm(qkv_c[..., GDN_H * GDN_DK:2 * GDN_H * GDN_DK].reshape(bsz, L, GDN_H, GDN_DK))
    vd = qkv_c[..., 2 * GDN_H * GDN_DK:].reshape(bsz, L, GDN_H, GDN_DV)
    beta_d = jax.nn.sigmoid(beta_logit.astype(f32))
    g_d = -jnp.exp(p['delta_a_log'].astype(f32)) * jax.nn.softplus(a_logit.astype(f32) + p['delta_dt_bias'].astype(f32))
    o_d, s_new = gated_delta_chunked(qd, kd, vd, g_d, beta_d, s0)
    o_d = (o_d * lax.rsqrt(jnp.mean(jnp.square(o_d), axis=-1, keepdims=True) + RMS_EPS)
           * p['delta_norm_w'].astype(f32) * jax.nn.silu(z_g.astype(f32).reshape(bsz, L, GDN_H, GDN_DV)))
    y_c = o_d.reshape(bsz, L, W_MIX).astype(x.dtype)

    y_d = attend(q_sb, k_sb, v_sb).reshape(bsz, L, W_MIX)

    branches = jnp.stack([y_a, y_b, y_c, y_d], axis=2)
    br = jnp.einsum('blnw,nwd->blnd', branches, p['w_branch'])
    gates = jax.nn.sigmoid(h @ p['w_gate']).reshape(bsz, L, N_BRANCH, D_MODEL)
    mixed = jnp.sum(gates * br, axis=2) @ p['w_o']
    x = layer_norm(ALPHA_DN * x + (1.0 + g1) * mixed, p['ln1_g'], p['ln1_b'])

    h2 = layer_norm(x) * (1.0 + sc2) + sh2
    up_a, up_b = jnp.split(h2 @ p['w_ffn_up'], 2, axis=-1)
    ffn = (jax.nn.silu(up_a) * up_b) @ p['w_ffn_down']
    x = layer_norm(ALPHA_DN * x + (1.0 + g2) * ffn, p['ln2_g'], p['ln2_b'])
    return x, (k_sb, v_sb, h_re, h_im, convb_new, s_new, convd_new)


def setup_inputs(seed: int = 0) -> dict:
    key = jax.random.key(seed)
    ks = iter(jax.random.split(key, 48))
    nxt = lambda: next(ks)
    f32 = jnp.float32
    n_pages = PAST_LEN // PAGE_SIZE
    n_used = DEC_BATCH * n_pages
    n_pool = n_used + max(1, n_used // 4)
    nrm = lambda shape, s: jax.random.normal(nxt(), shape, f32) * s

    x_prompt = nrm((BATCH, SEQ, D_MODEL), 1.0)
    x_sample = nrm((DEC_BATCH, DEC_SEQ, D_MODEL), 1.0)
    cache_k = nrm((DEPTH, n_pool, PAGE_SIZE, SB_H, SB_DH), 1.0)
    cache_v = nrm((DEPTH, n_pool, PAGE_SIZE, SB_H, SB_DH), 1.0)
    state_ssm_re = nrm((DEPTH, DEC_BATCH, SSM_GROUPS, SSM_P), 0.5)
    state_ssm_im = nrm((DEPTH, DEC_BATCH, SSM_GROUPS, SSM_P), 0.5)
    state_conv_b = nrm((DEPTH, DEC_BATCH, CONV_B_W - 1, W_CONV), 1.0)
    state_delta = nrm((DEPTH, DEC_BATCH, GDN_H, GDN_DK, GDN_DV), 0.1)
    state_conv_delta = nrm((DEPTH, DEC_BATCH, GDN_CONV - 1, W_GDN_QKV), 1.0)
    page_table = jax.random.permutation(nxt(), n_pool)[:n_used].reshape(DEC_BATCH, n_pages).astype(jnp.int32)
    c_prompt = nrm((BATCH, D_MODEL), 1.0)
    c_sample = nrm((DEC_BATCH, D_MODEL), 1.0)

    w_ada = nrm((DEPTH, D_MODEL, 6 * D_MODEL), 0.2 * D_MODEL ** -0.5)
    b_ada = nrm((DEPTH, 6 * D_MODEL), 0.01)
    w_in = nrm((DEPTH, D_MODEL, N_IN), D_MODEL ** -0.5)
    ssm_a_re = -0.5 + nrm((DEPTH, SSM_GROUPS, SSM_P), 0.01)
    ssm_a_im = jnp.pi * jnp.arange(SSM_P, dtype=f32) + nrm((DEPTH, SSM_GROUPS, SSM_P), 0.01)
    ssm_log_dt = jax.random.uniform(nxt(), (DEPTH, SSM_GROUPS), f32, math.log(1e-3), math.log(1e-1))
    ssm_b_re = nrm((DEPTH, SSM_GROUPS, SSM_P, SSM_GROUP), (2 * SSM_GROUP) ** -0.5)
    ssm_b_im = nrm((DEPTH, SSM_GROUPS, SSM_P, SSM_GROUP), (2 * SSM_GROUP) ** -0.5)
    ssm_c_re = nrm((DEPTH, SSM_GROUPS, SSM_GROUP, SSM_P), (2 * SSM_P) ** -0.5)
    ssm_c_im = nrm((DEPTH, SSM_GROUPS, SSM_GROUP, SSM_P), (2 * SSM_P) ** -0.5)
    ssm_d = nrm((DEPTH, W_SSM), 1.0)
    w_glu = nrm((DEPTH, W_SSM, W_SSM), W_SSM ** -0.5)
    conv_b_w = nrm((DEPTH, CONV_B_W, W_CONV), CONV_B_W ** -0.5)
    delta_conv_w = nrm((DEPTH, GDN_CONV, W_GDN_QKV), GDN_CONV ** -0.5)
    delta_a_log = jnp.log(jax.random.uniform(nxt(), (DEPTH, GDN_H), f32, 1.0, 16.0))
    dt0 = jnp.exp(jax.random.uniform(nxt(), (DEPTH, GDN_H), f32, math.log(1e-3), math.log(1e-1)))
    delta_dt_bias = jnp.log(jnp.expm1(dt0))
    delta_norm_w = 1.0 + nrm((DEPTH, GDN_DV), 0.01)
    w_branch = nrm((DEPTH, N_BRANCH, W_MIX, D_MODEL), W_MIX ** -0.5)
    w_gate = nrm((DEPTH, D_MODEL, N_BRANCH * D_MODEL), D_MODEL ** -0.5)
    w_o = nrm((DEPTH, D_MODEL, D_MODEL), BETA_DN * D_MODEL ** -0.5)
    ln1_g = 1.0 + nrm((DEPTH, D_MODEL), 0.01)
    ln1_b = nrm((DEPTH, D_MODEL), 0.01)
    w_ffn_up = nrm((DEPTH, D_MODEL, 2 * D_FF), D_MODEL ** -0.5)
    w_ffn_down = nrm((DEPTH, D_FF, D_MODEL), BETA_DN * D_FF ** -0.5)
    ln2_g = 1.0 + nrm((DEPTH, D_MODEL), 0.01)
    ln2_b = nrm((DEPTH, D_MODEL), 0.01)
    return {'x_prompt': x_prompt, 'x_sample': x_sample, 'cache_k': cache_k, 'cache_v': cache_v,
            'state_ssm_re': state_ssm_re, 'state_ssm_im': state_ssm_im, 'state_conv_b': state_conv_b,
            'state_delta': state_delta, 'state_conv_delta': state_conv_delta, 'page_table': page_table,
            'c_prompt': c_prompt, 'c_sample': c_sample, 'w_ada': w_ada, 'b_ada': b_ada, 'w_in': w_in,
            'ssm_a_re': ssm_a_re, 'ssm_a_im': ssm_a_im, 'ssm_log_dt': ssm_log_dt, 'ssm_b_re': ssm_b_re,
            'ssm_b_im': ssm_b_im, 'ssm_c_re': ssm_c_re, 'ssm_c_im': ssm_c_im, 'ssm_d': ssm_d, 'w_glu': w_glu,
            'conv_b_w': conv_b_w, 'delta_conv_w': delta_conv_w, 'delta_a_log': delta_a_log,
            'delta_dt_bias': delta_dt_bias, 'delta_norm_w': delta_norm_w, 'w_branch': w_branch, 'w_gate': w_gate,
            'w_o': w_o, 'ln1_g': ln1_g, 'ln1_b': ln1_b, 'w_ffn_up': w_ffn_up, 'w_ffn_down': w_ffn_down,
            'ln2_g': ln2_g, 'ln2_b': ln2_b}


def reference(x_prompt, x_sample, cache_k, cache_v, state_ssm_re, state_ssm_im, state_conv_b, state_delta,
              state_conv_delta, page_table, c_prompt, c_sample, w_ada, b_ada, w_in, ssm_a_re, ssm_a_im, ssm_log_dt,
              ssm_b_re, ssm_b_im, ssm_c_re, ssm_c_im, ssm_d, w_glu, conv_b_w, delta_conv_w, delta_a_log,
              delta_dt_bias, delta_norm_w, w_branch, w_gate, w_o, ln1_g, ln1_b, w_ffn_up, w_ffn_down, ln2_g, ln2_b):
    bp = x_prompt.shape[0]
    bs = x_sample.shape[0]
    dt_p = x_prompt.dtype
    x_p, x_s = x_prompt, x_sample
    prompt_new, sample_new = [], []
    for l in range(DEPTH):
        p = {'w_ada': w_ada[l], 'b_ada': b_ada[l], 'w_in': w_in[l], 'ssm_a_re': ssm_a_re[l], 'ssm_a_im': ssm_a_im[l],
             'ssm_log_dt': ssm_log_dt[l], 'ssm_b_re': ssm_b_re[l], 'ssm_b_im': ssm_b_im[l], 'ssm_c_re': ssm_c_re[l],
             'ssm_c_im': ssm_c_im[l], 'ssm_d': ssm_d[l], 'w_glu': w_glu[l], 'conv_b_w': conv_b_w[l],
             'delta_conv_w': delta_conv_w[l], 'delta_a_log': delta_a_log[l], 'delta_dt_bias': delta_dt_bias[l],
             'delta_norm_w': delta_norm_w[l], 'w_branch': w_branch[l], 'w_gate': w_gate[l], 'w_o': w_o[l],
             'ln1_g': ln1_g[l], 'ln1_b': ln1_b[l], 'w_ffn_up': w_ffn_up[l], 'w_ffn_down': w_ffn_down[l],
             'ln2_g': ln2_g[l], 'ln2_b': ln2_b[l]}
        st_p = (jnp.zeros((bp, SSM_GROUPS, SSM_P), dt_p), jnp.zeros((bp, SSM_GROUPS, SSM_P), dt_p),
                jnp.zeros((bp, CONV_B_W - 1, W_CONV), dt_p), jnp.zeros((bp, GDN_H, GDN_DK, GDN_DV), dt_p),
                jnp.zeros((bp, GDN_CONV - 1, W_GDN_QKV), dt_p))
        x_p, new_p = decoder_layer(x_p, c_prompt, p, st_p, sb_prompt)
        prompt_new.append(new_p)
        k_past = cache_k[l][page_table].reshape(bs, -1, SB_H, SB_DH)
        v_past = cache_v[l][page_table].reshape(bs, -1, SB_H, SB_DH)
        attend_s = functools.partial(sb_sample, k_past=k_past, v_past=v_past)
        st_s = (state_ssm_re[l], state_ssm_im[l], state_conv_b[l], state_delta[l], state_conv_delta[l])
        x_s, new_s = decoder_layer(x_s, c_sample, p, st_s, attend_s)
        sample_new.append(new_s)
    k_p, v_p, re_p, im_p, cb_p, d_p, cd_p = [jnp.stack(t, axis=0) for t in zip(*prompt_new)]
    k_s, v_s, re_s, im_s, cb_s, d_s, cd_s = [jnp.stack(t, axis=0) for t in zip(*sample_new)]
    return (x_p, x_s, k_p, v_p, k_s, v_s, re_p, im_p, re_s, im_s, cb_p, cb_s, d_p, d_s, cd_p, cd_s)
```

```python
import functools
import math

import jax
import jax.numpy as jnp
from jax import lax
from jax.experimental import pallas as pl
from jax.experimental.pallas import tpu as pltpu

F32 = jnp.float32
BF16 = jnp.bfloat16

D = 1024
W = 256
NH = 4
DH = 64
G = 16
GW = 16
P = 64
NS = G * P
DFF = 2816
CHUNK = 64
PAGE = 128
ALPHA = (2 * 2) ** 0.25
LN_EPS = 1e-5
RMS_EPS = 1e-6
VMEM_LIMIT = 56 * 1024 * 1024

SEGS = (("ssm", 256, 0, 256), ("conv", 768, 256, 768), ("qkv", 768, 1024, 768), ("z", 256, 1792, 256),
        ("ba", 128, 2048, 8), ("q", 256, 2056, 256), ("k", 256, 2312, 256), ("v", 256, 2568, 256))
NP_IN = sum(s[1] for s in SEGS)


def _cp(*sem):
    return pltpu.CompilerParams(dimension_semantics=sem, vmem_limit_bytes=VMEM_LIMIT)


def _dot(a, b):
    return jnp.dot(a, b, preferred_element_type=F32)


def _dot_nt(a, b):
    return lax.dot_general(a, b, (((1,), (1,)), ((), ())), preferred_element_type=F32)


def _split(x):
    hi = x.astype(BF16)
    lo = (x - hi.astype(F32)).astype(BF16)
    return hi, lo


def _dot_exact_rhs(x, m_bf16):
    hi, lo = _split(x)
    return _dot(hi, m_bf16) + _dot(lo, m_bf16)


def _mm3(a, b):
    ah, al = _split(a)
    bh, bl = _split(b)
    return _dot(ah, bh) + (_dot(ah, bl) + _dot(al, bh))


def _mm3_nt(a, b):
    ah, al = _split(a)
    bh, bl = _split(b)
    return _dot_nt(ah, bh) + (_dot_nt(ah, bl) + _dot_nt(al, bh))


def _ln(x):
    mu = jnp.mean(x, axis=-1, keepdims=True)
    xc = x - mu
    var = jnp.mean(xc * xc, axis=-1, keepdims=True)
    return xc * lax.rsqrt(var + LN_EPS)


def _softplus(x):
    return jnp.maximum(x, 0.0) + jnp.log1p(jnp.exp(-jnp.abs(x)))


def _silu(x):
    return x * jax.nn.sigmoid(x)


def _mod_kernel(c_ref, w_ref, b_ref, o_ref):
    s = _silu(c_ref[...]).astype(BF16)
    o_ref[0] = _dot(s, w_ref[0].astype(BF16)) + b_ref[0]


def _mod(c_all, w_ada, b_ada):
    mp = c_all.shape[0]
    depth = w_ada.shape[0]
    tn = 1024
    return pl.pallas_call(
        _mod_kernel, grid=(depth, 6 * D // tn),
        in_specs=[pl.BlockSpec((mp, D), lambda l, j: (0, 0)),
                  pl.BlockSpec((1, D, tn), lambda l, j: (l, 0, j)),
                  pl.BlockSpec((1, 1, tn), lambda l, j: (l, 0, j))],
        out_specs=pl.BlockSpec((1, mp, tn), lambda l, j: (l, 0, j)),
        out_shape=jax.ShapeDtypeStruct((depth, mp, 6 * D), F32),
        compiler_params=_cp("arbitrary", "arbitrary"), name="ada_mod",
    )(c_all, w_ada, b_ada.reshape(depth, 1, 6 * D))


def _mod_spec(mod, tm):
    if mod.shape[2] == 1:
        return pl.BlockSpec((None, 6, 1, D), lambda b, i: (b, 0, 0, 0))
    return pl.BlockSpec((None, 6, tm, D), lambda b, i: (b, 0, i, 0))


def _tok_spec(tm, c):
    return pl.BlockSpec((None, tm, c), lambda b, i: (b, i, 0))


def _full_spec(shape):
    n = len(shape)
    return pl.BlockSpec(shape, lambda b, i: (0,) * n)


def _proj_kernel(x_ref, mod_ref, w_ref, *outs):
    h = (_ln(x_ref[...]) * (1.0 + mod_ref[1]) + mod_ref[0]).astype(BF16)
    off = 0
    for o_ref, seg in zip(outs, SEGS):
        o_ref[...] = _dot(h, w_ref[:, off:off + seg[1]])
        off += seg[1]


def _proj(x, mod, w_in_p, tm):
    b, l, _ = x.shape
    return pl.pallas_call(
        _proj_kernel, grid=(b, l // tm),
        in_specs=[_tok_spec(tm, D), _mod_spec(mod, tm), _full_spec((D, NP_IN))],
        out_specs=[_tok_spec(tm, s[1]) for s in SEGS],
        out_shape=[jax.ShapeDtypeStruct((b, l, s[1]), F32) for s in SEGS],
        compiler_params=_cp("arbitrary", "arbitrary"), name="in_proj",
    )(x, mod, w_in_p)


def _merge_kernel(x_ref, mod_ref, ya_ref, yb_ref, yc_ref, yd_ref, wg_ref, wb_ref, wo_ref, lg_ref, lb_ref, o_ref):
    x = x_ref[...]
    h = (_ln(x) * (1.0 + mod_ref[1]) + mod_ref[0]).astype(BF16)
    acc = None
    for n, y_ref in enumerate((ya_ref, yb_ref, yc_ref, yd_ref)):
        gate = jax.nn.sigmoid(_dot(h, wg_ref[:, n * D:(n + 1) * D]))
        br = _dot(y_ref[...].astype(BF16), wb_ref[n])
        acc = gate * br if acc is None else acc + gate * br
    mixed = _dot(acc.astype(BF16), wo_ref[...])
    r = ALPHA * x + (1.0 + mod_ref[2]) * mixed
    o_ref[...] = _ln(r) * lg_ref[...] + lb_ref[...]


def _merge(x, mod, ya, yb, yc, yd, wg, wb, wo, lg, lb, tm):
    b, l, _ = x.shape
    return pl.pallas_call(
        _merge_kernel, grid=(b, l // tm),
        in_specs=[_tok_spec(tm, D), _mod_spec(mod, tm)] + [_tok_spec(tm, W)] * 4 +
                 [_full_spec((D, 4 * D)), _full_spec((4, W, D)), _full_spec((D, D)),
                  _full_spec((1, D)), _full_spec((1, D))],
        out_specs=_tok_spec(tm, D),
        out_shape=jax.ShapeDtypeStruct((b, l, D), F32),
        compiler_params=_cp("arbitrary", "arbitrary"), name="merge",
    )(x, mod, ya, yb, yc, yd, wg, wb, wo, lg, lb)


FF_CHUNK = 256


def _ffn_kernel(x_ref, mod_ref, wu_ref, wd_ref, lg_ref, lb_ref, o_ref):
    x = x_ref[...]
    h = (_ln(x) * (1.0 + mod_ref[4]) + mod_ref[3]).astype(BF16)
    acc = None
    for c in range(DFF // FF_CHUNK):
        lo = c * FF_CHUNK
        up_a = _dot(h, wu_ref[:, lo:lo + FF_CHUNK])
        up_b = _dot(h, wu_ref[:, DFF + lo:DFF + lo + FF_CHUNK])
        t = (_silu(up_a) * up_b).astype(BF16)
        d = _dot(t, wd_ref[lo:lo + FF_CHUNK, :])
        acc = d if acc is None else acc + d
    r = ALPHA * x + (1.0 + mod_ref[5]) * acc
    o_ref[...] = _ln(r) * lg_ref[...] + lb_ref[...]


def _ffn(x, mod, wu, wd, lg, lb, tm):
    b, l, _ = x.shape
    return pl.pallas_call(
        _ffn_kernel, grid=(b, l // tm),
        in_specs=[_tok_spec(tm, D), _mod_spec(mod, tm), _full_spec((D, 2 * DFF)), _full_spec((DFF, D)),
                  _full_spec((1, D)), _full_spec((1, D))],
        out_specs=_tok_spec(tm, D),
        out_shape=jax.ShapeDtypeStruct((b, l, D), F32),
        compiler_params=_cp("arbitrary", "arbitrary"), name="ffn",
    )(x, mod, wu, wd, lg, lb)


def _s5_prep_kernel(are_ref, aim_ref, ldt_ref, bre_ref, bim_ref, apow_ref, bbre_ref, bbim_ref):
    a_re = are_ref[...]
    a_im = aim_ref[...]
    dt = jnp.exp(ldt_ref[...])
    mag = jnp.exp(dt * a_re)
    ab_re = mag * jnp.cos(dt * a_im)
    ab_im = mag * jnp.sin(dt * a_im)
    den = a_re * a_re + a_im * a_im
    f_re = ((ab_re - 1.0) * a_re + ab_im * a_im) / den
    f_im = (ab_im * a_re - (ab_re - 1.0) * a_im) / den
    b_re = bre_ref[...]
    b_im = bim_ref[...]
    bbre_ref[...] = f_re * b_re - f_im * b_im
    bbim_ref[...] = f_re * b_im + f_im * b_re
    pr, pi = ab_re, ab_im
    for k in range(4):
        apow_ref[2 * k] = pr
        apow_ref[2 * k + 1] = pi
        pr, pi = pr * pr - pi * pi, 2.0 * pr * pi


def _s5_prep(a_re, a_im, log_dt, b_re, b_im):
    rep = lambda t: jnp.repeat(t, GW, axis=0)
    ldt = jnp.broadcast_to(log_dt[:, None], (G, P))
    bt = lambda t: jnp.transpose(t, (0, 2, 1)).reshape(G * GW, P)
    apow_x, bb_re, bb_im = pl.pallas_call(
        _s5_prep_kernel,
        out_shape=[jax.ShapeDtypeStruct((8, G * GW, P), F32), jax.ShapeDtypeStruct((G * GW, P), F32),
                   jax.ShapeDtypeStruct((G * GW, P), F32)], name="s5_prep",
    )(rep(a_re), rep(a_im), rep(ldt), bt(b_re), bt(b_im))
    apow = apow_x[:, ::GW, :].reshape(8, NS)
    eye = jnp.eye(G, dtype=F32)
    bd = lambda t: jnp.einsum("ghp,gk->ghkp", t.reshape(G, GW, P), eye).reshape(W, NS)
    return apow, jnp.concatenate([bd(bb_re), bd(bb_im)], axis=1)


def _s5_cmat(c):
    eye = jnp.eye(G, dtype=F32)
    return jnp.einsum("ghp,gk->gpkh", c, eye).reshape(NS, W)


def _s5_out(hr, hi, u, cre_ref, cim_ref, d_ref, wglu_ref):
    y = _dot(hr.astype(BF16), cre_ref[...]) - _dot(hi.astype(BF16), cim_ref[...]) + d_ref[...] * u
    ys = jax.nn.gelu(y)
    return ys * jax.nn.sigmoid(_dot(ys.astype(BF16), wglu_ref[...]))


S5_TM = 128
LANES = 128


def _s5_prompt_kernel(u_ref, apow_ref, bbd_ref, cre_ref, cim_ref, d_ref, wglu_ref,
                      ya_ref, hre_ref, him_ref, sbuf, hprev, hbuf):
    tm = S5_TM
    i = pl.program_id(1)

    @pl.when(i == 0)
    def _():
        sbuf[0:8, :] = jnp.zeros((8, 2 * NS), F32)
        hprev[...] = jnp.zeros((8, 2 * NS), F32)

    u = u_ref[...]
    sbuf[8:8 + tm, :] = _dot(u.astype(BF16), bbd_ref[...])
    for c in range(NS // LANES):
        cr = slice(c * LANES, (c + 1) * LANES)
        ci = slice(NS + c * LANES, NS + (c + 1) * LANES)
        sr = sbuf[:, cr]
        si = sbuf[:, ci]
        for k, shift in enumerate((1, 2, 4)):
            ar = apow_ref[2 * k:2 * k + 1, cr]
            ai = apow_ref[2 * k + 1:2 * k + 2, cr]
            pr = pltpu.roll(sr, shift, 0)
            pi = pltpu.roll(si, shift, 0)
            sr, si = sr + (ar * pr - ai * pi), si + (ar * pi + ai * pr)
        a8r = apow_ref[6:7, cr]
        a8i = apow_ref[7:8, cr]
        hr = hprev[:, cr]
        hi = hprev[:, ci]
        for k in range(tm // 8):
            wr = sr[8 + 8 * k:16 + 8 * k]
            wi = si[8 + 8 * k:16 + 8 * k]
            hr, hi = a8r * hr - a8i * hi + wr, a8r * hi + a8i * hr + wi
            hbuf[8 * k:8 * k + 8, cr] = hr
            hbuf[8 * k:8 * k + 8, ci] = hi
        hprev[:, cr] = hr
        hprev[:, ci] = hi
    sbuf[0:8, :] = sbuf[tm:tm + 8, :]
    ya_ref[...] = _s5_out(hbuf[:, :NS], hbuf[:, NS:], u, cre_ref, cim_ref, d_ref, wglu_ref)

    @pl.when(i == pl.num_programs(1) - 1)
    def _():
        hre_ref[...] = hprev[7:8, :NS]
        him_ref[...] = hprev[7:8, NS:]


def _s5_prompt(u, apow, bbd, cre, cim, d, wglu):
    b, l, _ = u.shape
    tm = S5_TM
    return pl.pallas_call(
        _s5_prompt_kernel, grid=(b, l // tm),
        in_specs=[_tok_spec(tm, W), _full_spec((8, NS)), _full_spec((W, 2 * NS)), _full_spec((NS, W)),
                  _full_spec((NS, W)), _full_spec((1, W)), _full_spec((W, W))],
        out_specs=[_tok_spec(tm, W), pl.BlockSpec((None, 1, NS), lambda b, i: (b, 0, 0)),
                   pl.BlockSpec((None, 1, NS), lambda b, i: (b, 0, 0))],
        out_shape=[jax.ShapeDtypeStruct((b, l, W), F32), jax.ShapeDtypeStruct((b, 1, NS), F32),
                   jax.ShapeDtypeStruct((b, 1, NS), F32)],
        scratch_shapes=[pltpu.VMEM((tm + 8, 2 * NS), F32), pltpu.VMEM((8, 2 * NS), F32),
                        pltpu.VMEM((tm, 2 * NS), F32)],
        compiler_params=_cp("arbitrary", "arbitrary"), name="s5_prompt",
    )(u, apow, bbd, cre, cim, d, wglu)


def _s5_sample_kernel(u_ref, h0re_ref, h0im_ref, apow_ref, bbd_ref, cre_ref, cim_ref, d_ref, wglu_ref,
                      ya_ref, hre_ref, him_ref):
    ar = apow_ref[0:1, :]
    ai = apow_ref[1:2, :]
    hr = h0re_ref[...]
    hi = h0im_ref[...]
    for t in range(u_ref.shape[0]):
        u = u_ref[t]
        bu = _dot(u.astype(BF16), bbd_ref[...])
        hr, hi = ar * hr - ai * hi + bu[:, :NS], ar * hi + ai * hr + bu[:, NS:]
        ya_ref[t] = _s5_out(hr, hi, u, cre_ref, cim_ref, d_ref, wglu_ref)
    hre_ref[...] = hr
    him_ref[...] = hi


def _s5_sample(u, h0re, h0im, apow, bbd, cre, cim, d, wglu):
    t, s, _ = u.shape
    return pl.pallas_call(
        _s5_sample_kernel,
        out_shape=[jax.ShapeDtypeStruct((t, s, W), F32), jax.ShapeDtypeStruct((s, NS), F32),
                   jax.ShapeDtypeStruct((s, NS), F32)],
        compiler_params=pltpu.CompilerParams(vmem_limit_bytes=VMEM_LIMIT), name="s5_sample",
    )(u, h0re, h0im, apow, bbd, cre, cim, d, wglu)


def _head_sumsq(x, ones_ref):
    return _dot_exact_rhs(x * x, ones_ref[...])


def _delta_act(conv, ba, gp_ref, ones_ref):
    a = _silu(conv)
    q = a[:, :W]
    k = a[:, W:2 * W]
    v = a[:, 2 * W:]
    q = q * lax.rsqrt(_head_sumsq(q, ones_ref) + RMS_EPS) * (DH ** -0.5)
    k = k * lax.rsqrt(_head_sumsq(k, ones_ref) + RMS_EPS)
    lane = lax.broadcasted_iota(jnp.int32, ba.shape, 1)
    beta = jax.nn.sigmoid(ba)
    g = -jnp.exp(gp_ref[0:1, :]) * _softplus(ba + gp_ref[1:2, :])
    gb = jnp.where(lane < NH, beta, jnp.where(lane < 2 * NH, g, 0.0))
    return q, k, v, gb


CONV_TM = 256


def _convprep_prompt_kernel(c3_ref, qkv_ref, ba_ref, wb_ref, wd_ref, gp_ref, ones_ref,
                            yb_ref, q_ref, k_ref, v_ref, gb_ref, cbs_ref, cds_ref, cbuf, dbuf):
    tm = CONV_TM
    i = pl.program_id(1)

    @pl.when(i == 0)
    def _():
        cbuf[0:8, :] = jnp.zeros((8, W), F32)
        dbuf[0:8, :] = jnp.zeros((8, 3 * W), F32)

    c3 = c3_ref[...]
    cx = c3[:, W:2 * W] * c3[:, 2 * W:]
    cbuf[8:8 + tm, :] = cx
    full = cbuf[...]
    conv = wb_ref[2:3, :] * cx
    for j in (1, 2):
        conv = conv + wb_ref[2 - j:3 - j, :] * pltpu.roll(full, j, 0)[8:]
    yb_ref[...] = c3[:, :W] * conv
    cbs_ref[...] = cbuf[tm:tm + 8, :]
    cbuf[0:8, :] = cbuf[tm:tm + 8, :]

    x = qkv_ref[...]
    dbuf[8:8 + tm, :] = x
    full = dbuf[...]
    conv = wd_ref[3:4, :] * x
    for j in (1, 2, 3):
        conv = conv + wd_ref[3 - j:4 - j, :] * pltpu.roll(full, j, 0)[8:]
    cds_ref[...] = dbuf[tm:tm + 8, :]
    dbuf[0:8, :] = dbuf[tm:tm + 8, :]
    q, k, v, gb = _delta_act(conv, ba_ref[...], gp_ref, ones_ref)
    q_ref[...] = q
    k_ref[...] = k
    v_ref[...] = v
    gb_ref[...] = gb


def _convprep_prompt(c3, qkv, ba, wb, wd, gp, ones_bd):
    b, l, _ = c3.shape
    tm = CONV_TM
    st = lambda c: pl.BlockSpec((None, 8, c), lambda b, i: (b, 0, 0))
    return pl.pallas_call(
        _convprep_prompt_kernel, grid=(b, l // tm),
        in_specs=[_tok_spec(tm, 3 * W), _tok_spec(tm, 3 * W), _tok_spec(tm, 128), _full_spec((3, W)),
                  _full_spec((4, 3 * W)), _full_spec((2, 128)), _full_spec((W, W))],
        out_specs=[_tok_spec(tm, W)] * 4 + [_tok_spec(tm, 128), st(W), st(3 * W)],
        out_shape=[jax.ShapeDtypeStruct((b, l, W), F32)] * 4 +
                  [jax.ShapeDtypeStruct((b, l, 128), F32), jax.ShapeDtypeStruct((b, 8, W), F32),
                   jax.ShapeDtypeStruct((b, 8, 3 * W), F32)],
        scratch_shapes=[pltpu.VMEM((tm + 8, W), F32), pltpu.VMEM((tm + 8, 3 * W), F32)],
        compiler_params=_cp("arbitrary", "arbitrary"), name="convprep_prompt",
    )(c3, qkv, ba, wb, wd, gp, ones_bd)


GDN_CPG = 2


def _tri_inv(m, ii, jj):
    eye = (ii == jj).astype(F32)
    blk = (ii // 16) == (jj // 16)
    nd = jnp.where(blk, -m, 0.0)
    e = jnp.where(blk, 0.0, m)
    n2 = _mm3(nd, nd)
    n4 = _mm3(n2, n2)
    n8 = _mm3(n4, n4)
    dinv = _mm3(_mm3(_mm3(eye + nd, eye + n2), eye + n4), eye + n8)
    x = _mm3(dinv, e)
    x2 = _mm3(x, x)
    return _mm3(_mm3(eye - x, eye + x2), dinv)


def _gdn_local_kernel(q_ref, k_ref, v_ref, gb_ref, uv_ref, w_ref, qk_ref, qd_ref, kdt_ref, egl_ref):
    cs = CHUNK
    ii = lax.broadcasted_iota(jnp.int32, (cs, cs), 0)
    jj = lax.broadcasted_iota(jnp.int32, (cs, cs), 1)
    tri_incl = (ii >= jj).astype(BF16)
    tri_up = (ii <= jj).astype(BF16)
    for c in range(GDN_CPG):
        rows = slice(c * cs, (c + 1) * cs)
        gbc = gb_ref[rows, :]
        hi, lo = _split(gbc)
        gc_all = _dot(tri_incl, hi) + _dot(tri_incl, lo)
        gct_all = _dot_exact_rhs(gbc.T, tri_up)
        qc = q_ref[rows, :]
        kc = k_ref[rows, :]
        vc = v_ref[rows, :]
        uvs, ws, qks, qds, kdts, egls = [], [], [], [], [], []
        for h in range(NH):
            hs = slice(h * DH, (h + 1) * DH)
            gcol = gc_all[:, NH + h:NH + h + 1]
            grow = gct_all[NH + h:NH + h + 1, :]
            beta = gbc[:, h:h + 1]
            diff = gcol - grow
            e = jnp.exp(jnp.where(ii >= jj, diff, 0.0))
            dec_incl = jnp.where(ii >= jj, e, 0.0)
            dec_strict = jnp.where(ii > jj, e, 0.0)
            qh = qc[:, hs]
            kh = kc[:, hs]
            vh = vc[:, hs]
            kk = _mm3_nt(kh, kh)
            qkr = _mm3_nt(qh, kh)
            m = beta * kk * dec_strict
            egc = jnp.exp(gcol)
            rhs = jnp.concatenate([beta * vh, (beta * egc) * kh], axis=1)
            sol = _mm3(_tri_inv(m, ii, jj), rhs)
            glast = gcol[cs - 1:cs, :]
            uvs.append(sol[:, :DH])
            ws.append(sol[:, DH:])
            qks.append(qkr * dec_incl)
            qds.append(qh * egc)
            kdts.append((kh * jnp.exp(glast - gcol)).T)
            egls.append(jnp.broadcast_to(jnp.exp(glast), (8, DH)))
        cat = lambda ts: jnp.concatenate(ts, axis=1)
        uv_ref[rows, :] = cat(uvs)
        w_ref[rows, :] = cat(ws)
        qk_ref[rows, :] = cat(qks)
        qd_ref[rows, :] = cat(qds)
        kdt_ref[rows, :] = cat(kdts)
        egl_ref[c] = cat(egls)


def _gdn_local(q, k, v, gb):
    b, l, _ = q.shape
    r = GDN_CPG * CHUNK
    nc = l // CHUNK
    return pl.pallas_call(
        _gdn_local_kernel, grid=(b, l // r),
        in_specs=[_tok_spec(r, W)] * 3 + [_tok_spec(r, 128)],
        out_specs=[_tok_spec(r, W)] * 5 + [pl.BlockSpec((None, GDN_CPG, 8, W), lambda b, i: (b, i, 0, 0))],
        out_shape=[jax.ShapeDtypeStruct((b, l, W), F32)] * 5 + [jax.ShapeDtypeStruct((b, nc, 8, W), F32)],
        compiler_params=_cp("arbitrary", "arbitrary"), name="gdn_local",
    )(q, k, v, gb)


def _gdn_scan_kernel(uv_ref, w_ref, qk_ref, qd_ref, kdt_ref, egl_ref, z_ref, nw_ref, y_ref, s_ref, s_acc):
    i = pl.program_id(1)

    @pl.when(i == 0)
    def _():
        s_acc[...] = jnp.zeros(s_acc.shape, F32)

    z = z_ref[...]
    outs = []
    for h in range(NH):
        hs = slice(h * DH, (h + 1) * DH)
        s = s_acc[h]
        u = uv_ref[:, hs] - _mm3(w_ref[:, hs], s)
        o = _mm3(qd_ref[:, hs], s) + _mm3(qk_ref[:, hs], u)
        s_acc[h] = egl_ref[0:1, hs] * s + _mm3(kdt_ref[:, hs], u)
        o = o * lax.rsqrt(jnp.mean(o * o, axis=-1, keepdims=True) + RMS_EPS) * nw_ref[...] * _silu(z[:, hs])
        outs.append(o)
    y_ref[...] = jnp.concatenate(outs, axis=1)

    @pl.when(i == pl.num_programs(1) - 1)
    def _():
        s_ref[...] = s_acc[...]


def _gdn_scan(uv, w, qk, qd, kdt, egl, z, nw):
    b, l, _ = uv.shape
    cs = CHUNK
    return pl.pallas_call(
        _gdn_scan_kernel, grid=(b, l // cs),
        in_specs=[_tok_spec(cs, W)] * 5 + [pl.BlockSpec((None, None, 8, W), lambda b, i: (b, i, 0, 0)),
                                           _tok_spec(cs, W), _full_spec((1, DH))],
        out_specs=[_tok_spec(cs, W), pl.BlockSpec((None, NH, DH, DH), lambda b, i: (b, 0, 0, 0))],
        out_shape=[jax.ShapeDtypeStruct((b, l, W), F32), jax.ShapeDtypeStruct((b, NH, DH, DH), F32)],
        scratch_shapes=[pltpu.VMEM((NH, DH, DH), F32)],
        compiler_params=_cp("arbitrary", "arbitrary"), name="gdn_scan",
    )(uv, w, qk, qd, kdt, egl, z, nw)


SB_TQ = 128
SB_TK = 128


def _sb_block(z, r, u_ref, mask):
    sp = _softplus(z)
    ln = -sp
    if mask is not None:
        ln = jnp.where(mask, ln, 0.0)
    e = _dot_exact_rhs(ln, u_ref[...])
    w = jnp.exp((z - sp) + e + r)
    if mask is not None:
        w = jnp.where(mask, w, 0.0)
    return w, r + jnp.sum(ln, axis=-1, keepdims=True)


def _sb_prompt_kernel(q_ref, kt_ref, v_ref, u_ref, o_ref):
    i = pl.program_id(2)
    q = (q_ref[...] * (DH ** -0.5)).astype(BF16)
    ii = lax.broadcasted_iota(jnp.int32, (SB_TQ, SB_TK), 0)
    jj = lax.broadcasted_iota(jnp.int32, (SB_TQ, SB_TK), 1)
    start = pl.multiple_of(i * SB_TK, SB_TK)
    z = _dot(q, kt_ref[:, pl.ds(start, SB_TK)])
    w, r = _sb_block(z, jnp.zeros((SB_TQ, 1), F32), u_ref, ii > jj)
    acc = _dot(w.astype(BF16), v_ref[pl.ds(start, SB_TK), :])

    def body(t, carry):
        acc, r = carry
        start = pl.multiple_of((i - 1 - t) * SB_TK, SB_TK)
        z = _dot(q, kt_ref[:, pl.ds(start, SB_TK)])
        w, r = _sb_block(z, r, u_ref, None)
        return acc + _dot(w.astype(BF16), v_ref[pl.ds(start, SB_TK), :]), r

    acc, _ = lax.fori_loop(0, i, body, (acc, r))
    o_ref[...] = acc


def _sb_prompt(qh, kt, vh, u_strict):
    b, nh, l, dh = qh.shape
    return pl.pallas_call(
        _sb_prompt_kernel, grid=(b, nh, l // SB_TQ),
        in_specs=[pl.BlockSpec((None, None, SB_TQ, dh), lambda b, h, i: (b, h, i, 0)),
                  pl.BlockSpec((None, None, dh, l), lambda b, h, i: (b, h, 0, 0)),
                  pl.BlockSpec((None, None, l, dh), lambda b, h, i: (b, h, 0, 0)),
                  pl.BlockSpec((SB_TK, SB_TK), lambda b, h, i: (0, 0))],
        out_specs=pl.BlockSpec((None, None, SB_TQ, dh), lambda b, h, i: (b, h, i, 0)),
        out_shape=jax.ShapeDtypeStruct((b, nh, l, dh), F32),
        compiler_params=_cp("arbitrary", "arbitrary", "arbitrary"), name="sb_prompt",
    )(qh, kt, vh, u_strict)


def _strict_upper(n):
    i = jnp.arange(n)
    return (i[:, None] > i[None, :]).astype(BF16)


def _convprep_sample_kernel(c3_ref, qkv_ref, ba_ref, cb0_ref, cd0_ref, wb_ref, wd_ref, gp_ref, ones_ref,
                            yb_ref, qt_ref, kt_ref, vt_ref, gbt_ref, cbs_ref, cds_ref):
    nt = c3_ref.shape[0]
    xb = [cb0_ref[0], cb0_ref[1]]
    xd = [cd0_ref[0], cd0_ref[1], cd0_ref[2]]
    for t in range(nt):
        c3 = c3_ref[t]
        xb.append(c3[:, W:2 * W] * c3[:, 2 * W:])
        xd.append(qkv_ref[t])
    for t in range(nt):
        conv = xb[t] * wb_ref[0:1, :] + xb[t + 1] * wb_ref[1:2, :] + xb[t + 2] * wb_ref[2:3, :]
        yb_ref[t] = c3_ref[t][:, :W] * conv
        conv = (xd[t] * wd_ref[0:1, :] + xd[t + 1] * wd_ref[1:2, :] + xd[t + 2] * wd_ref[2:3, :]
                + xd[t + 3] * wd_ref[3:4, :])
        q, k, v, gb = _delta_act(conv, ba_ref[t], gp_ref, ones_ref)
        qt_ref[t] = q.T
        kt_ref[t] = k.T
        vt_ref[t] = v.T
        gbt_ref[t] = gb.T[0:8, :]
    cbs_ref[0] = xb[nt]
    cbs_ref[1] = xb[nt + 1]
    for j in range(3):
        cds_ref[j] = xd[nt + j]


def _convprep_sample(c3, qkv, ba, cb0, cd0, wb, wd, gp, ones_bd):
    t, s, _ = c3.shape
    sds = jax.ShapeDtypeStruct
    return pl.pallas_call(
        _convprep_sample_kernel,
        out_shape=[sds((t, s, W), F32), sds((t, W, s), F32), sds((t, W, s), F32), sds((t, W, s), F32),
                   sds((t, 8, s), F32), sds((2, s, W), F32), sds((3, s, 3 * W), F32)],
        compiler_params=pltpu.CompilerParams(vmem_limit_bytes=VMEM_LIMIT), name="convprep_sample",
    )(c3, qkv, ba, cb0, cd0, wb, wd, gp, ones_bd)


def _gdn_sample_kernel(qt_ref, kt_ref, vt_ref, gbt_ref, s0_ref, ot_ref, s1_ref):
    h = pl.program_id(0)
    nt = qt_ref.shape[0]
    ns = qt_ref.shape[2]
    for t in range(nt):
        src = s0_ref if t == 0 else s1_ref
        b = gbt_ref[t, pl.ds(h, 1), :]
        a = jnp.exp(gbt_ref[t, pl.ds(NH + h, 1), :])

        def ks_body(i, acc):
            return acc + kt_ref[t, pl.ds(i, 1), :] * src[i]

        ks = lax.fori_loop(0, DH, ks_body, jnp.zeros((DH, ns), F32), unroll=8)
        u = b * (vt_ref[t] - a * ks)

        def up_body(i, acc):
            sn = a * src[i] + kt_ref[t, pl.ds(i, 1), :] * u
            s1_ref[i] = sn
            return acc + qt_ref[t, pl.ds(i, 1), :] * sn

        ot_ref[t] = lax.fori_loop(0, DH, up_body, jnp.zeros((DH, ns), F32), unroll=8)


def _gdn_sample(qt, kt, vt, gbt, s0):
    t, _, s = qt.shape
    hb = pl.BlockSpec((t, DH, s), lambda h: (0, h, 0))
    sb = pl.BlockSpec((None, DH, DH, s), lambda h: (h, 0, 0, 0))
    return pl.pallas_call(
        _gdn_sample_kernel, grid=(NH,),
        in_specs=[hb, hb, hb, pl.BlockSpec((t, 8, s), lambda h: (0, 0, 0)), sb],
        out_specs=[hb, sb],
        out_shape=[jax.ShapeDtypeStruct((t, W, s), F32), jax.ShapeDtypeStruct((NH, DH, DH, s), F32)],
        compiler_params=_cp("arbitrary"), name="gdn_sample",
    )(qt, kt, vt, gbt, s0)


def _gdn_post_sample_kernel(ot_ref, z_ref, nw_ref, ones_ref, y_ref):
    for t in range(ot_ref.shape[0]):
        o = ot_ref[t].T
        ms = _head_sumsq(o, ones_ref) * (1.0 / DH)
        y_ref[t] = o * lax.rsqrt(ms + RMS_EPS) * nw_ref[...] * _silu(z_ref[t])


def _gdn_post_sample(ot, z, nw4, ones_bd):
    t, _, s = ot.shape
    return pl.pallas_call(
        _gdn_post_sample_kernel, out_shape=jax.ShapeDtypeStruct((t, s, W), F32), name="gdn_post_sample",
    )(ot, z, nw4, ones_bd)


SB_PPS = 8


def _sb_sample_kernel(pt_ref, q_ref, kn_ref, vn_ref, *rest):
    kp = rest[:SB_PPS]
    vp = rest[SB_PPS:2 * SB_PPS]
    u_ref, o_ref, acc_ref, r_ref, kpad, vpad = rest[2 * SB_PPS:]
    j = pl.program_id(1)
    nt = q_ref.shape[0]
    nq = NH * nt
    row = lax.broadcasted_iota(jnp.int32, (nq, W), 0)
    lane = lax.broadcasted_iota(jnp.int32, (nq, W), 1)
    q = q_ref[...] * (DH ** -0.5)
    qbd = jnp.where(row // nt == lane // DH, jnp.concatenate([q] * NH, axis=0), 0.0).astype(BF16)

    def block(k, v, r, acc, mask):
        z = _dot_nt(qbd, k.astype(BF16))
        w, r = _sb_block(z, r, u_ref, mask)
        return r, acc + _dot(w.astype(BF16), v.astype(BF16))

    @pl.when(j == 0)
    def _():
        kpad[...] = jnp.zeros(kpad.shape, F32)
        vpad[...] = jnp.zeros(vpad.shape, F32)
        kpad[0:nt, :] = kn_ref[...]
        vpad[0:nt, :] = vn_ref[...]
        qi = lax.broadcasted_iota(jnp.int32, (nq, PAGE), 0) % nt
        ki = lax.broadcasted_iota(jnp.int32, (nq, PAGE), 1)
        r, acc = block(kpad[...], vpad[...], jnp.zeros((nq, 1), F32), jnp.zeros((nq, W), F32), ki < qi)
        r_ref[...] = r
        acc_ref[...] = acc

    r = r_ref[...]
    acc = acc_ref[...]
    for p in range(SB_PPS - 1, -1, -1):
        r, acc = block(kp[p][...], vp[p][...], r, acc, None)
    r_ref[...] = r
    acc_ref[...] = acc

    @pl.when(j == pl.num_programs(1) - 1)
    def _():
        sel = jnp.where(row // nt == lane // DH, acc, 0.0)
        out = sel[0:nt]
        for h in range(1, NH):
            out = out + sel[h * nt:(h + 1) * nt]
        o_ref[...] = out


def _sb_sample(page_table, q, kn, vn, cache_k, cache_v, layer, u_strict):
    s, t, _ = q.shape
    n_pages = page_table.shape[1]
    ng = n_pages // SB_PPS
    seq = pl.BlockSpec((None, t, W), lambda s_, j, pt: (s_, 0, 0))

    def page_spec(p):
        return pl.BlockSpec((None, None, PAGE, W),
                            lambda s_, j, pt: (layer, pt[s_, (ng - 1 - j) * SB_PPS + p], 0, 0))

    grid_spec = pltpu.PrefetchScalarGridSpec(
        num_scalar_prefetch=1, grid=(s, ng),
        in_specs=[seq, seq, seq] + [page_spec(p) for p in range(SB_PPS)] * 2 +
                 [pl.BlockSpec((PAGE, PAGE), lambda s_, j, pt: (0, 0))],
        out_specs=seq,
        scratch_shapes=[pltpu.VMEM((NH * t, W), F32), pltpu.VMEM((NH * t, 1), F32),
                        pltpu.VMEM((PAGE, W), F32), pltpu.VMEM((PAGE, W), F32)])
    return pl.pallas_call(
        _sb_sample_kernel, grid_spec=grid_spec,
        out_shape=jax.ShapeDtypeStruct((s, t, W), F32),
        compiler_params=_cp("arbitrary", "arbitrary"), name="sb_sample",
    )(page_table, q, kn, vn, *([cache_k] * SB_PPS), *([cache_v] * SB_PPS), u_strict)


def _layer_weights(l, w_in, ssm_a_re, ssm_a_im, ssm_log_dt, ssm_b_re, ssm_b_im, ssm_c_re, ssm_c_im, ssm_d, w_glu,
                   conv_b_w, delta_conv_w, delta_a_log, delta_dt_bias, delta_norm_w, w_branch, w_gate, w_o,
                   ln1_g, ln1_b, w_ffn_up, w_ffn_down, ln2_g, ln2_b):
    w = w_in[l]
    w_in_p = jnp.concatenate([w[:, :2048], jnp.pad(w[:, 2048:2056], ((0, 0), (0, 120))), w[:, 2056:]],
                             axis=1).astype(BF16)
    apow, bbd = _s5_prep(ssm_a_re[l], ssm_a_im[l], ssm_log_dt[l], ssm_b_re[l], ssm_b_im[l])
    gp = jnp.zeros((2, 128), F32)
    gp = gp.at[0, NH:2 * NH].set(delta_a_log[l]).at[1, NH:2 * NH].set(delta_dt_bias[l])
    hid = jnp.arange(W) // DH
    return dict(
        w_in=w_in_p, apow=apow, bbd=bbd.astype(BF16), cre=_s5_cmat(ssm_c_re[l]).astype(BF16),
        cim=_s5_cmat(ssm_c_im[l]).astype(BF16), d=ssm_d[l].reshape(1, W), wglu=w_glu[l].astype(BF16),
        wb=conv_b_w[l], wd=delta_conv_w[l], gp=gp, nw=delta_norm_w[l].reshape(1, DH),
        nw4=jnp.tile(delta_norm_w[l], NH).reshape(1, W),
        ones_bd=(hid[:, None] == hid[None, :]).astype(BF16),
        wbr=w_branch[l].astype(BF16), wg=w_gate[l].astype(BF16), wo=w_o[l].astype(BF16),
        ln1g=ln1_g[l].reshape(1, D), ln1b=ln1_b[l].reshape(1, D),
        wu=w_ffn_up[l].astype(BF16), wdn=w_ffn_down[l].astype(BF16),
        ln2g=ln2_g[l].reshape(1, D), ln2b=ln2_b[l].reshape(1, D))


def _prompt_layer(x, mod, p, u_strict):
    b, l, _ = x.shape
    u, c3, qkv, z, ba, q, k, v = _proj(x, mod, p["w_in"], 512)
    ya, hre, him = _s5_prompt(u, p["apow"], p["bbd"], p["cre"], p["cim"], p["d"], p["wglu"])
    yb, qd, kd, vd, gb, cbs, cds = _convprep_prompt(c3, qkv, ba, p["wb"], p["wd"], p["gp"], p["ones_bd"])
    uv, w, qk, qdec, kdt, egl = _gdn_local(qd, kd, vd, gb)
    yc, s_new = _gdn_scan(uv, w, qk, qdec, kdt, egl, z, p["nw"])
    heads = lambda t: t.reshape(b, l, NH, DH).transpose(0, 2, 1, 3)
    o = _sb_prompt(heads(q), heads(k).transpose(0, 1, 3, 2).astype(BF16), heads(v).astype(BF16), u_strict)
    yd = o.transpose(0, 2, 1, 3).reshape(b, l, W)
    x1 = _merge(x, mod, ya, yb, yc, yd, p["wg"], p["wbr"], p["wo"], p["ln1g"], p["ln1b"], 256)
    x2 = _ffn(x1, mod, p["wu"], p["wdn"], p["ln2g"], p["ln2b"], 512)
    new = (k.reshape(b, l, NH, DH), v.reshape(b, l, NH, DH), hre.reshape(b, G, P), him.reshape(b, G, P),
           cbs[:, 6:8], s_new, cds[:, 5:8])
    return x2, new


def _sample_layer(x, mod, p, st, page_table, cache_k, cache_v, layer, u_strict):
    h0re, h0im, cb0, s0, cd0 = st
    s = h0re.shape[0]
    n = x.shape[1]
    t = n // s
    u, c3, qkv, z, ba, q, k, v = _proj(x, mod, p["w_in"], n)
    tm = lambda a: a.reshape(t, s, a.shape[-1])
    sm = lambda a: tm(a).transpose(1, 0, 2)
    ya, hre, him = _s5_sample(tm(u), h0re.reshape(s, NS), h0im.reshape(s, NS), p["apow"], p["bbd"], p["cre"],
                              p["cim"], p["d"], p["wglu"])
    yb, qt, kt, vt, gbt, cbs, cds = _convprep_sample(tm(c3), tm(qkv), tm(ba), cb0.transpose(1, 0, 2),
                                                      cd0.transpose(1, 0, 2), p["wb"], p["wd"], p["gp"],
                                                      p["ones_bd"])
    ot, s1 = _gdn_sample(qt, kt, vt, gbt, s0.transpose(1, 2, 3, 0))
    yc = _gdn_post_sample(ot, tm(z), p["nw4"], p["ones_bd"])
    ksm, vsm = sm(k), sm(v)
    yd = _sb_sample(page_table, sm(q), ksm, vsm, cache_k, cache_v, layer, u_strict).transpose(1, 0, 2)
    flat = lambda a: a.reshape(1, n, W)
    x1 = _merge(x, mod, flat(ya), flat(yb), flat(yc), flat(yd), p["wg"], p["wbr"], p["wo"], p["ln1g"], p["ln1b"],
                min(256, n))
    x2 = _ffn(x1, mod, p["wu"], p["wdn"], p["ln2g"], p["ln2b"], n)
    new = (ksm.reshape(s, t, NH, DH), vsm.reshape(s, t, NH, DH), hre.reshape(s, G, P), him.reshape(s, G, P),
           cbs.transpose(1, 0, 2), s1.transpose(3, 0, 1, 2), cds.transpose(1, 0, 2))
    return x2, new


def kernel(x_prompt, x_sample, cache_k, cache_v, state_ssm_re, state_ssm_im, state_conv_b, state_delta,
           state_conv_delta, page_table, c_prompt, c_sample, w_ada, b_ada, w_in, ssm_a_re, ssm_a_im, ssm_log_dt,
           ssm_b_re, ssm_b_im, ssm_c_re, ssm_c_im, ssm_d, w_glu, conv_b_w, delta_conv_w, delta_a_log,
           delta_dt_bias, delta_norm_w, w_branch, w_gate, w_o, ln1_g, ln1_b, w_ffn_up, w_ffn_down, ln2_g, ln2_b):
    depth = w_ada.shape[0]
    bp, lp, _ = x_prompt.shape
    bs, ts, _ = x_sample.shape
    n_pool = cache_k.shape[1]

    c_all = jnp.concatenate([c_prompt, c_sample], axis=0)
    pad = (-c_all.shape[0]) % 8
    mod = _mod(jnp.pad(c_all, ((0, pad), (0, 0))), w_ada, b_ada)
    mod_p = mod[:, :bp].reshape(depth, bp, 6, 1, D)
    mod_s = mod[:, bp:bp + bs].reshape(depth, bs, 6, D).transpose(0, 2, 1, 3)
    mod_s = jnp.broadcast_to(mod_s[:, :, None], (depth, 6, ts, bs, D)).reshape(depth, 1, 6, ts * bs, D)

    ck = cache_k.reshape(depth, n_pool, PAGE, W)
    cv = cache_v.reshape(depth, n_pool, PAGE, W)
    u_strict = _strict_upper(PAGE)

    x_p = x_prompt
    x_s = x_sample.transpose(1, 0, 2).reshape(1, ts * bs, D)
    new_p, new_s = [], []
    for l in range(depth):
        p = _layer_weights(l, w_in, ssm_a_re, ssm_a_im, ssm_log_dt, ssm_b_re, ssm_b_im, ssm_c_re, ssm_c_im, ssm_d,
                           w_glu, conv_b_w, delta_conv_w, delta_a_log, delta_dt_bias, delta_norm_w, w_branch,
                           w_gate, w_o, ln1_g, ln1_b, w_ffn_up, w_ffn_down, ln2_g, ln2_b)
        x_p, np_ = _prompt_layer(x_p, mod_p[l], p, u_strict)
        new_p.append(np_)
        st = (state_ssm_re[l], state_ssm_im[l], state_conv_b[l], state_delta[l], state_conv_delta[l])
        x_s, ns_ = _sample_layer(x_s, mod_s[l], p, st, page_table, ck, cv, l, u_strict)
        new_s.append(ns_)
    k_p, v_p, re_p, im_p, cb_p, d_p, cd_p = [jnp.stack(t, axis=0) for t in zip(*new_p)]
    k_s, v_s, re_s, im_s, cb_s, d_s, cd_s = [jnp.stack(t, axis=0) for t in zip(*new_s)]
    y_s = x_s.reshape(ts, bs, D).transpose(1, 0, 2)
    return (x_p, y_s, k_p, v_p, k_s, v_s, re_p, im_p, re_s, im_s, cb_p, cb_s, d_p, d_s, cd_p, cd_s)
```

```python
import functools
import math

import jax
import jax.numpy as jnp
from jax import lax
from jax.experimental import pallas as pl
from jax.experimental.pallas import tpu as pltpu

F32 = jnp.float32
BF16 = jnp.bfloat16

D = 1024
W = 256
NH = 4
DH = 64
G = 16
GW = 16
P = 64
NS = G * P
DFF = 2816
CHUNK = 64
PAGE = 128
ALPHA = (2 * 2) ** 0.25
LN_EPS = 1e-5
RMS_EPS = 1e-6
LOG2E = 1.4426950408889634
VMEM_LIMIT = 56 * 1024 * 1024

SEGS = (("ssm", 256, 0, 256), ("conv", 768, 256, 768), ("qkv", 768, 1024, 768), ("z", 256, 1792, 256),
        ("ba", 128, 2048, 8), ("q", 256, 2056, 256), ("k", 256, 2312, 256), ("v", 256, 2568, 256))
NP_IN = sum(s[1] for s in SEGS)


def _cp(*sem):
    return pltpu.CompilerParams(dimension_semantics=sem, vmem_limit_bytes=VMEM_LIMIT)


def _dot(a, b):
    return jnp.dot(a, b, preferred_element_type=F32)


def _dot_nt(a, b):
    return lax.dot_general(a, b, (((1,), (1,)), ((), ())), preferred_element_type=F32)


def _split(x):
    hi = x.astype(BF16)
    lo = (x - hi.astype(F32)).astype(BF16)
    return hi, lo


def _dot_exact_rhs(x, m_bf16):
    hi, lo = _split(x)
    return _dot(hi, m_bf16) + _dot(lo, m_bf16)


def _mm3(a, b):
    ah, al = _split(a)
    bh, bl = _split(b)
    return _dot(ah, bh) + (_dot(ah, bl) + _dot(al, bh))


def _mm3_nt(a, b):
    ah, al = _split(a)
    bh, bl = _split(b)
    return _dot_nt(ah, bh) + (_dot_nt(ah, bl) + _dot_nt(al, bh))


def _ln(x):
    mu = jnp.mean(x, axis=-1, keepdims=True)
    xc = x - mu
    var = jnp.mean(xc * xc, axis=-1, keepdims=True)
    return xc * lax.rsqrt(var + LN_EPS)


def _softplus(x):
    return jnp.maximum(x, 0.0) + jnp.log1p(jnp.exp(-jnp.abs(x)))


def _silu(x):
    return x * jax.nn.sigmoid(x)


def _mod_kernel(c_ref, w_ref, b_ref, o_ref):
    s = _silu(c_ref[...]).astype(BF16)
    o_ref[0] = _dot(s, w_ref[0].astype(BF16)) + b_ref[0]


def _mod(c_all, w_ada, b_ada):
    mp = c_all.shape[0]
    depth = w_ada.shape[0]
    tn = 1024
    return pl.pallas_call(
        _mod_kernel, grid=(depth, 6 * D // tn),
        in_specs=[pl.BlockSpec((mp, D), lambda l, j: (0, 0)),
                  pl.BlockSpec((1, D, tn), lambda l, j: (l, 0, j)),
                  pl.BlockSpec((1, 1, tn), lambda l, j: (l, 0, j))],
        out_specs=pl.BlockSpec((1, mp, tn), lambda l, j: (l, 0, j)),
        out_shape=jax.ShapeDtypeStruct((depth, mp, 6 * D), F32),
        compiler_params=_cp("arbitrary", "arbitrary"), name="ada_mod",
    )(c_all, w_ada, b_ada.reshape(depth, 1, 6 * D))


def _mod_spec(mod, tm):
    if mod.shape[2] == 1:
        return pl.BlockSpec((None, 6, 1, D), lambda b, i: (b, 0, 0, 0))
    return pl.BlockSpec((None, 6, tm, D), lambda b, i: (b, 0, i, 0))


def _tok_spec(tm, c):
    return pl.BlockSpec((None, tm, c), lambda b, i: (b, i, 0))


def _full_spec(shape):
    n = len(shape)
    return pl.BlockSpec(shape, lambda b, i: (0,) * n)


def _proj_kernel(x_ref, mod_ref, w_ref, *outs):
    h = (_ln(x_ref[...]) * (1.0 + mod_ref[1]) + mod_ref[0]).astype(BF16)
    off = 0
    seg_out = {}
    for o_ref, seg in zip(outs, SEGS):
        seg_out[seg[0]] = _dot(h, w_ref[:, off:off + seg[1]])
        o_ref[...] = seg_out[seg[0]]
        off += seg[1]
    if len(outs) > len(SEGS):
        k16_ref, vt16_ref = outs[len(SEGS):]
        k16_ref[...] = seg_out["k"].astype(BF16)
        vt16_ref[...] = seg_out["v"].T.astype(BF16)


def _proj(x, mod, w_in_p, tm, attn_copies):
    b, l, _ = x.shape
    out_specs = [_tok_spec(tm, s[1]) for s in SEGS]
    out_shape = [jax.ShapeDtypeStruct((b, l, s[1]), F32) for s in SEGS]
    if attn_copies:
        out_specs += [_tok_spec(tm, W), pl.BlockSpec((None, W, tm), lambda b, i: (b, 0, i))]
        out_shape += [jax.ShapeDtypeStruct((b, l, W), BF16), jax.ShapeDtypeStruct((b, W, l), BF16)]
    return pl.pallas_call(
        _proj_kernel, grid=(b, l // tm),
        in_specs=[_tok_spec(tm, D), _mod_spec(mod, tm), _full_spec((D, NP_IN))],
        out_specs=out_specs, out_shape=out_shape,
        compiler_params=_cp("arbitrary", "arbitrary"), name="in_proj",
    )(x, mod, w_in_p)


def _merge_kernel(x_ref, mod_ref, ya_ref, yb_ref, yc_ref, yd_ref, wg_ref, wb_ref, wo_ref, lg_ref, lb_ref, o_ref):
    x = x_ref[...]
    h = (_ln(x) * (1.0 + mod_ref[1]) + mod_ref[0]).astype(BF16)
    acc = None
    for n, y_ref in enumerate((ya_ref, yb_ref, yc_ref, yd_ref)):
        gate = jax.nn.sigmoid(_dot(h, wg_ref[:, n * D:(n + 1) * D]))
        br = _dot(y_ref[...].astype(BF16), wb_ref[n])
        acc = gate * br if acc is None else acc + gate * br
    mixed = _dot(acc.astype(BF16), wo_ref[...])
    r = ALPHA * x + (1.0 + mod_ref[2]) * mixed
    o_ref[...] = _ln(r) * lg_ref[...] + lb_ref[...]


def _merge(x, mod, ya, yb, yc, yd, wg, wb, wo, lg, lb, tm):
    b, l, _ = x.shape
    return pl.pallas_call(
        _merge_kernel, grid=(b, l // tm),
        in_specs=[_tok_spec(tm, D), _mod_spec(mod, tm)] + [_tok_spec(tm, W)] * 4 +
                 [_full_spec((D, 4 * D)), _full_spec((4, W, D)), _full_spec((D, D)),
                  _full_spec((1, D)), _full_spec((1, D))],
        out_specs=_tok_spec(tm, D),
        out_shape=jax.ShapeDtypeStruct((b, l, D), F32),
        compiler_params=_cp("arbitrary", "arbitrary"), name="merge",
    )(x, mod, ya, yb, yc, yd, wg, wb, wo, lg, lb)


FF_CHUNK = 256


def _ffn_kernel(x_ref, mod_ref, wu_ref, wd_ref, lg_ref, lb_ref, o_ref):
    x = x_ref[...]
    h = (_ln(x) * (1.0 + mod_ref[4]) + mod_ref[3]).astype(BF16)
    acc = None
    for c in range(DFF // FF_CHUNK):
        lo = c * FF_CHUNK
        up_a = _dot(h, wu_ref[:, lo:lo + FF_CHUNK])
        up_b = _dot(h, wu_ref[:, DFF + lo:DFF + lo + FF_CHUNK])
        t = (_silu(up_a) * up_b).astype(BF16)
        d = _dot(t, wd_ref[lo:lo + FF_CHUNK, :])
        acc = d if acc is None else acc + d
    r = ALPHA * x + (1.0 + mod_ref[5]) * acc
    o_ref[...] = _ln(r) * lg_ref[...] + lb_ref[...]


def _ffn(x, mod, wu, wd, lg, lb, tm):
    b, l, _ = x.shape
    return pl.pallas_call(
        _ffn_kernel, grid=(b, l // tm),
        in_specs=[_tok_spec(tm, D), _mod_spec(mod, tm), _full_spec((D, 2 * DFF)), _full_spec((DFF, D)),
                  _full_spec((1, D)), _full_spec((1, D))],
        out_specs=_tok_spec(tm, D),
        out_shape=jax.ShapeDtypeStruct((b, l, D), F32),
        compiler_params=_cp("arbitrary", "arbitrary"), name="ffn",
    )(x, mod, wu, wd, lg, lb)


def _s5_prep_kernel(are_ref, aim_ref, ldt_ref, bre_ref, bim_ref, apow_ref, bbre_ref, bbim_ref):
    a_re = are_ref[...]
    a_im = aim_ref[...]
    dt = jnp.exp(ldt_ref[...])
    mag = jnp.exp(dt * a_re)
    ab_re = mag * jnp.cos(dt * a_im)
    ab_im = mag * jnp.sin(dt * a_im)
    den = a_re * a_re + a_im * a_im
    f_re = ((ab_re - 1.0) * a_re + ab_im * a_im) / den
    f_im = (ab_im * a_re - (ab_re - 1.0) * a_im) / den
    b_re = bre_ref[...]
    b_im = bim_ref[...]
    bbre_ref[...] = f_re * b_re - f_im * b_im
    bbim_ref[...] = f_re * b_im + f_im * b_re
    pr, pi = ab_re, ab_im
    for k in range(4):
        apow_ref[2 * k] = pr
        apow_ref[2 * k + 1] = pi
        pr, pi = pr * pr - pi * pi, 2.0 * pr * pi


def _s5_prep(a_re, a_im, log_dt, b_re, b_im):
    rep = lambda t: jnp.repeat(t, GW, axis=0)
    ldt = jnp.broadcast_to(log_dt[:, None], (G, P))
    bt = lambda t: jnp.transpose(t, (0, 2, 1)).reshape(G * GW, P)
    apow_x, bb_re, bb_im = pl.pallas_call(
        _s5_prep_kernel,
        out_shape=[jax.ShapeDtypeStruct((8, G * GW, P), F32), jax.ShapeDtypeStruct((G * GW, P), F32),
                   jax.ShapeDtypeStruct((G * GW, P), F32)], name="s5_prep",
    )(rep(a_re), rep(a_im), rep(ldt), bt(b_re), bt(b_im))
    apow = apow_x[:, ::GW, :].reshape(8, NS)
    eye = jnp.eye(G, dtype=F32)
    bd = lambda t: jnp.einsum("ghp,gk->ghkp", t.reshape(G, GW, P), eye).reshape(W, NS)
    return apow, jnp.concatenate([bd(bb_re), bd(bb_im)], axis=1)


def _s5_cmat(c):
    eye = jnp.eye(G, dtype=F32)
    return jnp.einsum("ghp,gk->gpkh", c, eye).reshape(NS, W)


def _s5_out(hr, hi, u, cre_ref, cim_ref, d_ref, wglu_ref):
    y = _dot(hr.astype(BF16), cre_ref[...]) - _dot(hi.astype(BF16), cim_ref[...]) + d_ref[...] * u
    ys = jax.nn.gelu(y)
    return ys * jax.nn.sigmoid(_dot(ys.astype(BF16), wglu_ref[...]))


S5_TM = 128
LANES = 128


def _s5_prompt_kernel(u_ref, apow_ref, bbd_ref, cre_ref, cim_ref, d_ref, wglu_ref,
                      ya_ref, hre_ref, him_ref, sbuf, hprev, hbuf):
    tm = S5_TM
    i = pl.program_id(1)

    @pl.when(i == 0)
    def _():
        sbuf[0:8, :] = jnp.zeros((8, 2 * NS), F32)
        hprev[...] = jnp.zeros((8, 2 * NS), F32)

    u = u_ref[...]
    sbuf[8:8 + tm, :] = _dot(u.astype(BF16), bbd_ref[...])
    for c in range(NS // LANES):
        cr = slice(c * LANES, (c + 1) * LANES)
        ci = slice(NS + c * LANES, NS + (c + 1) * LANES)
        sr = sbuf[:, cr]
        si = sbuf[:, ci]
        for k, shift in enumerate((1, 2, 4)):
            ar = apow_ref[2 * k:2 * k + 1, cr]
            ai = apow_ref[2 * k + 1:2 * k + 2, cr]
            pr = pltpu.roll(sr, shift, 0)
            pi = pltpu.roll(si, shift, 0)
            sr, si = sr + (ar * pr - ai * pi), si + (ar * pi + ai * pr)
        a8r = apow_ref[6:7, cr]
        a8i = apow_ref[7:8, cr]
        hr = hprev[:, cr]
        hi = hprev[:, ci]
        for k in range(tm // 8):
            wr = sr[8 + 8 * k:16 + 8 * k]
            wi = si[8 + 8 * k:16 + 8 * k]
            hr, hi = a8r * hr - a8i * hi + wr, a8r * hi + a8i * hr + wi
            hbuf[8 * k:8 * k + 8, cr] = hr
            hbuf[8 * k:8 * k + 8, ci] = hi
        hprev[:, cr] = hr
        hprev[:, ci] = hi
    sbuf[0:8, :] = sbuf[tm:tm + 8, :]
    ya_ref[...] = _s5_out(hbuf[:, :NS], hbuf[:, NS:], u, cre_ref, cim_ref, d_ref, wglu_ref)

    @pl.when(i == pl.num_programs(1) - 1)
    def _():
        hre_ref[...] = hprev[7:8, :NS]
        him_ref[...] = hprev[7:8, NS:]


def _s5_prompt(u, apow, bbd, cre, cim, d, wglu):
    b, l, _ = u.shape
    tm = S5_TM
    return pl.pallas_call(
        _s5_prompt_kernel, grid=(b, l // tm),
        in_specs=[_tok_spec(tm, W), _full_spec((8, NS)), _full_spec((W, 2 * NS)), _full_spec((NS, W)),
                  _full_spec((NS, W)), _full_spec((1, W)), _full_spec((W, W))],
        out_specs=[_tok_spec(tm, W), pl.BlockSpec((None, 1, NS), lambda b, i: (b, 0, 0)),
                   pl.BlockSpec((None, 1, NS), lambda b, i: (b, 0, 0))],
        out_shape=[jax.ShapeDtypeStruct((b, l, W), F32), jax.ShapeDtypeStruct((b, 1, NS), F32),
                   jax.ShapeDtypeStruct((b, 1, NS), F32)],
        scratch_shapes=[pltpu.VMEM((tm + 8, 2 * NS), F32), pltpu.VMEM((8, 2 * NS), F32),
                        pltpu.VMEM((tm, 2 * NS), F32)],
        compiler_params=_cp("arbitrary", "arbitrary"), name="s5_prompt",
    )(u, apow, bbd, cre, cim, d, wglu)


def _s5_sample_kernel(u_ref, h0re_ref, h0im_ref, apow_ref, bbd_ref, cre_ref, cim_ref, d_ref, wglu_ref,
                      ya_ref, hre_ref, him_ref):
    ar = apow_ref[0:1, :]
    ai = apow_ref[1:2, :]
    hr = h0re_ref[...]
    hi = h0im_ref[...]
    for t in range(u_ref.shape[0]):
        u = u_ref[t]
        bu = _dot(u.astype(BF16), bbd_ref[...])
        hr, hi = ar * hr - ai * hi + bu[:, :NS], ar * hi + ai * hr + bu[:, NS:]
        ya_ref[t] = _s5_out(hr, hi, u, cre_ref, cim_ref, d_ref, wglu_ref)
    hre_ref[...] = hr
    him_ref[...] = hi


def _s5_sample(u, h0re, h0im, apow, bbd, cre, cim, d, wglu):
    t, s, _ = u.shape
    return pl.pallas_call(
        _s5_sample_kernel,
        out_shape=[jax.ShapeDtypeStruct((t, s, W), F32), jax.ShapeDtypeStruct((s, NS), F32),
                   jax.ShapeDtypeStruct((s, NS), F32)],
        compiler_params=pltpu.CompilerParams(vmem_limit_bytes=VMEM_LIMIT), name="s5_sample",
    )(u, h0re, h0im, apow, bbd, cre, cim, d, wglu)


def _head_sumsq(x, ones_ref):
    return _dot_exact_rhs(x * x, ones_ref[...])


def _delta_act(conv, ba, gp_ref, ones_ref):
    a = _silu(conv)
    q = a[:, :W]
    k = a[:, W:2 * W]
    v = a[:, 2 * W:]
    q = q * lax.rsqrt(_head_sumsq(q, ones_ref) + RMS_EPS) * (DH ** -0.5)
    k = k * lax.rsqrt(_head_sumsq(k, ones_ref) + RMS_EPS)
    lane = lax.broadcasted_iota(jnp.int32, ba.shape, 1)
    beta = jax.nn.sigmoid(ba)
    g = -jnp.exp(gp_ref[0:1, :]) * _softplus(ba + gp_ref[1:2, :])
    gb = jnp.where(lane < NH, beta, jnp.where(lane < 2 * NH, g, 0.0))
    return q, k, v, gb


CONV_TM = 256


def _convprep_prompt_kernel(c3_ref, qkv_ref, ba_ref, wb_ref, wd_ref, gp_ref, ones_ref,
                            yb_ref, q_ref, k_ref, v_ref, gb_ref, cbs_ref, cds_ref, cbuf, dbuf):
    tm = CONV_TM
    i = pl.program_id(1)

    @pl.when(i == 0)
    def _():
        cbuf[0:8, :] = jnp.zeros((8, W), F32)
        dbuf[0:8, :] = jnp.zeros((8, 3 * W), F32)

    c3 = c3_ref[...]
    cx = c3[:, W:2 * W] * c3[:, 2 * W:]
    cbuf[8:8 + tm, :] = cx
    full = cbuf[...]
    conv = wb_ref[2:3, :] * cx
    for j in (1, 2):
        conv = conv + wb_ref[2 - j:3 - j, :] * pltpu.roll(full, j, 0)[8:]
    yb_ref[...] = c3[:, :W] * conv
    cbs_ref[...] = cbuf[tm:tm + 8, :]
    cbuf[0:8, :] = cbuf[tm:tm + 8, :]

    x = qkv_ref[...]
    dbuf[8:8 + tm, :] = x
    full = dbuf[...]
    conv = wd_ref[3:4, :] * x
    for j in (1, 2, 3):
        conv = conv + wd_ref[3 - j:4 - j, :] * pltpu.roll(full, j, 0)[8:]
    cds_ref[...] = dbuf[tm:tm + 8, :]
    dbuf[0:8, :] = dbuf[tm:tm + 8, :]
    q, k, v, gb = _delta_act(conv, ba_ref[...], gp_ref, ones_ref)
    q_ref[...] = q
    k_ref[...] = k
    v_ref[...] = v
    gb_ref[...] = gb


def _convprep_prompt(c3, qkv, ba, wb, wd, gp, ones_bd):
    b, l, _ = c3.shape
    tm = CONV_TM
    st = lambda c: pl.BlockSpec((None, 8, c), lambda b, i: (b, 0, 0))
    return pl.pallas_call(
        _convprep_prompt_kernel, grid=(b, l // tm),
        in_specs=[_tok_spec(tm, 3 * W), _tok_spec(tm, 3 * W), _tok_spec(tm, 128), _full_spec((3, W)),
                  _full_spec((4, 3 * W)), _full_spec((2, 128)), _full_spec((W, W))],
        out_specs=[_tok_spec(tm, W)] * 4 + [_tok_spec(tm, 128), st(W), st(3 * W)],
        out_shape=[jax.ShapeDtypeStruct((b, l, W), F32)] * 4 +
                  [jax.ShapeDtypeStruct((b, l, 128), F32), jax.ShapeDtypeStruct((b, 8, W), F32),
                   jax.ShapeDtypeStruct((b, 8, 3 * W), F32)],
        scratch_shapes=[pltpu.VMEM((tm + 8, W), F32), pltpu.VMEM((tm + 8, 3 * W), F32)],
        compiler_params=_cp("arbitrary", "arbitrary"), name="convprep_prompt",
    )(c3, qkv, ba, wb, wd, gp, ones_bd)


GDN_CPG = 2


def _each(fn, *lists):
    return [fn(*args) for args in zip(*lists)]


def _tri_inv_all(ms, ii, jj):
    eye = (ii == jj).astype(F32)
    blk = (ii // 16) == (jj // 16)
    nd = _each(lambda m: jnp.where(blk, -m, 0.0), ms)
    e = _each(lambda m: jnp.where(blk, 0.0, m), ms)
    n2 = _each(lambda a: _mm3(a, a), nd)
    n4 = _each(lambda a: _mm3(a, a), n2)
    n8 = _each(lambda a: _mm3(a, a), n4)
    d = _each(lambda a, b: _mm3(eye + a, eye + b), nd, n2)
    d = _each(lambda a, b: _mm3(a, eye + b), d, n4)
    dinv = _each(lambda a, b: _mm3(a, eye + b), d, n8)
    x = _each(_mm3, dinv, e)
    x2 = _each(lambda a: _mm3(a, a), x)
    t = _each(lambda a, b: _mm3(eye - a, eye + b), x, x2)
    return _each(_mm3, t, dinv)


def _gdn_local_kernel(q_ref, k_ref, v_ref, gb_ref, uv_ref, w_ref, qk_ref, qd_ref, kdt_ref, egl_ref):
    cs = CHUNK
    ii = lax.broadcasted_iota(jnp.int32, (cs, cs), 0)
    jj = lax.broadcasted_iota(jnp.int32, (cs, cs), 1)
    tri_incl = (ii >= jj).astype(BF16)
    tri_up = (ii <= jj).astype(BF16)
    rows = [slice(c * cs, (c + 1) * cs) for c in range(GDN_CPG)]
    gbc = [gb_ref[r, :] for r in rows]
    gc_all = _each(lambda g: sum(_dot(tri_incl, p) for p in _split(g)), gbc)
    gct_all = _each(lambda g: _dot_exact_rhs(g.T, tri_up), gbc)
    chains = [(c, h) for c in range(GDN_CPG) for h in range(NH)]
    hsl = lambda h: slice(h * DH, (h + 1) * DH)
    gcol = [gc_all[c][:, NH + h:NH + h + 1] for c, h in chains]
    grow = [gct_all[c][NH + h:NH + h + 1, :] for c, h in chains]
    beta = [gbc[c][:, h:h + 1] for c, h in chains]
    qc, kc, vc = ([ref[r, :] for r in rows] for ref in (q_ref, k_ref, v_ref))
    qh = [qc[c][:, hsl(h)] for c, h in chains]
    kh = [kc[c][:, hsl(h)] for c, h in chains]
    vh = [vc[c][:, hsl(h)] for c, h in chains]
    e = _each(lambda a, b: jnp.exp(jnp.where(ii >= jj, a - b, 0.0)), gcol, grow)
    kk = _each(_mm3_nt, kh, kh)
    qkr = _each(_mm3_nt, qh, kh)
    m = _each(lambda b, a, d: b * a * jnp.where(ii > jj, d, 0.0), beta, kk, e)
    egc = _each(jnp.exp, gcol)
    rhs = _each(lambda b, v, g, k: jnp.concatenate([b * v, (b * g) * k], axis=1), beta, vh, egc, kh)
    sol = _each(_mm3, _tri_inv_all(m, ii, jj), rhs)
    glast = [g[cs - 1:cs, :] for g in gcol]
    qk = _each(lambda a, d: a * jnp.where(ii >= jj, d, 0.0), qkr, e)
    qd = _each(lambda a, g: a * g, qh, egc)
    kdt = _each(lambda k, gl, g: (k * jnp.exp(gl - g)).T, kh, glast, gcol)
    egl = _each(lambda gl: jnp.broadcast_to(jnp.exp(gl), (8, DH)), glast)
    for c in range(GDN_CPG):
        cat = lambda ts: jnp.concatenate(ts[c * NH:(c + 1) * NH], axis=1)
        uv_ref[rows[c], :] = cat([s[:, :DH] for s in sol])
        w_ref[rows[c], :] = cat([s[:, DH:] for s in sol])
        qk_ref[rows[c], :] = cat(qk)
        qd_ref[rows[c], :] = cat(qd)
        kdt_ref[rows[c], :] = cat(kdt)
        egl_ref[c] = cat(egl)


def _gdn_local(q, k, v, gb):
    b, l, _ = q.shape
    r = GDN_CPG * CHUNK
    nc = l // CHUNK
    return pl.pallas_call(
        _gdn_local_kernel, grid=(b, l // r),
        in_specs=[_tok_spec(r, W)] * 3 + [_tok_spec(r, 128)],
        out_specs=[_tok_spec(r, W)] * 5 + [pl.BlockSpec((None, GDN_CPG, 8, W), lambda b, i: (b, i, 0, 0))],
        out_shape=[jax.ShapeDtypeStruct((b, l, W), F32)] * 5 + [jax.ShapeDtypeStruct((b, nc, 8, W), F32)],
        compiler_params=_cp("arbitrary", "arbitrary"), name="gdn_local",
    )(q, k, v, gb)


def _gdn_scan_kernel(uv_ref, w_ref, qk_ref, qd_ref, kdt_ref, egl_ref, z_ref, nw_ref, y_ref, s_ref, s_acc):
    i = pl.program_id(0)
    nb = uv_ref.shape[0]

    @pl.when(i == 0)
    def _():
        s_acc[...] = jnp.zeros(s_acc.shape, F32)

    chains = [(b, h) for b in range(nb) for h in range(NH)]
    hsl = lambda h: slice(h * DH, (h + 1) * DH)
    load = lambda ref: [ref[b][:, hsl(h)] for b, h in chains]
    s = [s_acc[b, h] for b, h in chains]
    u = _each(lambda a, w, st: a - _mm3(w, st), load(uv_ref), load(w_ref), s)
    o = _each(_mm3, load(qd_ref), s)
    o = _each(lambda a, qk, ut: a + _mm3(qk, ut), o, load(qk_ref), u)
    ks = _each(_mm3, load(kdt_ref), u)
    for (b, h), st, k in zip(chains, s, ks):
        s_acc[b, h] = egl_ref[b, 0, 0:1, hsl(h)] * st + k
    zs = load(z_ref)
    o = _each(lambda a, z: a * lax.rsqrt(jnp.mean(a * a, axis=-1, keepdims=True) + RMS_EPS) * nw_ref[...] * _silu(z),
              o, zs)
    for b in range(nb):
        y_ref[b] = jnp.concatenate(o[b * NH:(b + 1) * NH], axis=1)

    @pl.when(i == pl.num_programs(0) - 1)
    def _():
        s_ref[...] = s_acc[...]


def _gdn_scan(uv, w, qk, qd, kdt, egl, z, nw):
    b, l, _ = uv.shape
    cs = CHUNK
    blk = pl.BlockSpec((b, cs, W), lambda i: (0, i, 0))
    return pl.pallas_call(
        _gdn_scan_kernel, grid=(l // cs,),
        in_specs=[blk] * 5 + [pl.BlockSpec((b, 1, 8, W), lambda i: (0, i, 0, 0)), blk,
                              pl.BlockSpec((1, DH), lambda i: (0, 0))],
        out_specs=[blk, pl.BlockSpec((b, NH, DH, DH), lambda i: (0, 0, 0, 0))],
        out_shape=[jax.ShapeDtypeStruct((b, l, W), F32), jax.ShapeDtypeStruct((b, NH, DH, DH), F32)],
        scratch_shapes=[pltpu.VMEM((b, NH, DH, DH), F32)],
        compiler_params=_cp("arbitrary"), name="gdn_scan",
    )(uv, w, qk, qd, kdt, egl, z, nw)


SB_TQ = 256
SB_TK = 128


def _sb_block(z, r, u_ref, mask):
    sp = _softplus(z)
    ln = -sp
    if mask is not None:
        ln = jnp.where(mask, ln, 0.0)
    e = _dot_exact_rhs(ln, u_ref[...])
    w = jnp.exp((z - sp) + e + r)
    if mask is not None:
        w = jnp.where(mask, w, 0.0)
    return w, r + jnp.sum(ln, axis=-1, keepdims=True)


def _sb_prompt_kernel(q_ref, k_ref, vt_ref, a_ref, o_ref, nz_buf, et_buf, wt_buf):
    i = pl.program_id(1)
    tq, tk = SB_TQ, SB_TK
    qt = (q_ref[...] * (-(DH ** -0.5) * LOG2E)).T
    rowh = lax.broadcasted_iota(jnp.int32, (W, tq), 0) // DH
    qbd = jnp.concatenate([jnp.where(rowh == h, qt, 0.0) for h in range(NH)], axis=1).astype(BF16)
    kk = lax.broadcasted_iota(jnp.int32, (tk, NH * tq), 0)
    qq = lax.broadcasted_iota(jnp.int32, (tk, NH * tq), 1) % tq

    def logits(j):
        start = pl.multiple_of(jnp.maximum(j, 0) * tk, tk)
        return _dot(k_ref[pl.ds(start, tk), :], qbd)

    def suffix(nz, mask=None):
        ln = jnp.minimum(nz, 0.0) - jnp.log2(1.0 + jnp.exp2(-jnp.abs(nz)))
        if mask is not None:
            ln = jnp.where(mask, ln, 0.0)
        hi = ln.astype(BF16)
        lo = (ln - hi.astype(F32)).astype(BF16)
        return _dot(a_ref[...], jnp.concatenate([hi, lo], axis=0))

    def weights(nz, et, r, mask=None):
        wt = jnp.exp2((et - nz) + r)
        if mask is not None:
            wt = jnp.where(mask, wt, 0.0)
        return wt.astype(BF16)

    def attend(j, wt):
        vt = vt_ref[:, pl.ds(pl.multiple_of(j * tk, tk), tk)]
        return jnp.concatenate([_dot(vt[h * DH:(h + 1) * DH, :], wt[:, h * tq:(h + 1) * tq])
                                for h in range(NH)], axis=0)

    nslot = lambda j: lax.rem(j + 9, 3)
    eslot = lambda j: lax.rem(j + 4, 2)
    nd = tq // tk
    acc = jnp.zeros((W, tq), F32)
    r = jnp.zeros((1, NH * tq), F32)
    for d in range(nd):
        jd = (i + 1) * nd - 1 - d
        causal = (jd * tk + kk) < (i * tq + qq)
        nz = logits(jd)
        et = suffix(nz, causal)
        acc = acc + attend(jd, weights(nz, et, r, causal))
        r = r + et[0:1, :]
    j0 = i * nd - 1
    nz = logits(j0)
    et = suffix(nz)
    wt_buf[eslot(j0)] = weights(nz, et, r)
    r = r + et[0:1, :]
    nz = logits(j0 - 1)
    nz_buf[nslot(j0 - 1)] = nz
    et_buf[eslot(j0 - 1)] = suffix(nz)
    nz_buf[nslot(j0 - 2)] = logits(j0 - 2)

    def body(t, carry):
        acc, r = carry
        j = j0 - t
        p = attend(j, wt_buf[eslot(j)])
        nzz = logits(j - 3)
        etl = suffix(nz_buf[nslot(j - 2)])
        etx = et_buf[eslot(j - 1)]
        wtx = weights(nz_buf[nslot(j - 1)], etx, r)
        et_buf[eslot(j - 2)] = etl
        nz_buf[nslot(j - 3)] = nzz
        wt_buf[eslot(j - 1)] = wtx
        return acc + p, r + etx[0:1, :]

    acc, _ = lax.fori_loop(0, i * nd, body, (acc, r))
    o_ref[...] = acc.T


def _sb_prompt(q, k16, vt16, a_strict):
    b, l, _ = q.shape
    return pl.pallas_call(
        _sb_prompt_kernel, grid=(b, l // SB_TQ),
        in_specs=[_tok_spec(SB_TQ, W),
                  pl.BlockSpec((None, l, W), lambda b, i: (b, 0, 0)),
                  pl.BlockSpec((None, W, l), lambda b, i: (b, 0, 0)),
                  _full_spec((SB_TK, 2 * SB_TK))],
        out_specs=_tok_spec(SB_TQ, W),
        out_shape=jax.ShapeDtypeStruct((b, l, W), F32),
        scratch_shapes=[pltpu.VMEM((3, SB_TK, NH * SB_TQ), F32), pltpu.VMEM((2, SB_TK, NH * SB_TQ), F32),
                        pltpu.VMEM((2, SB_TK, NH * SB_TQ), BF16)],
        compiler_params=_cp("arbitrary", "arbitrary"), name="sb_prompt",
    )(q, k16, vt16, a_strict)


def _strict_upper(n):
    i = jnp.arange(n)
    return (i[:, None] > i[None, :]).astype(BF16)


def _convprep_sample_kernel(c3_ref, qkv_ref, ba_ref, cb0_ref, cd0_ref, wb_ref, wd_ref, gp_ref, ones_ref,
                            yb_ref, qt_ref, kt_ref, vt_ref, gbt_ref, cbs_ref, cds_ref):
    nt = c3_ref.shape[0]
    xb = [cb0_ref[0], cb0_ref[1]]
    xd = [cd0_ref[0], cd0_ref[1], cd0_ref[2]]
    for t in range(nt):
        c3 = c3_ref[t]
        xb.append(c3[:, W:2 * W] * c3[:, 2 * W:])
        xd.append(qkv_ref[t])
    for t in range(nt):
        conv = xb[t] * wb_ref[0:1, :] + xb[t + 1] * wb_ref[1:2, :] + xb[t + 2] * wb_ref[2:3, :]
        yb_ref[t] = c3_ref[t][:, :W] * conv
        conv = (xd[t] * wd_ref[0:1, :] + xd[t + 1] * wd_ref[1:2, :] + xd[t + 2] * wd_ref[2:3, :]
                + xd[t + 3] * wd_ref[3:4, :])
        q, k, v, gb = _delta_act(conv, ba_ref[t], gp_ref, ones_ref)
        qt_ref[t] = q.T
        kt_ref[t] = k.T
        vt_ref[t] = v.T
        gbt_ref[t] = gb.T[0:8, :]
    cbs_ref[0] = xb[nt]
    cbs_ref[1] = xb[nt + 1]
    for j in range(3):
        cds_ref[j] = xd[nt + j]


def _convprep_sample(c3, qkv, ba, cb0, cd0, wb, wd, gp, ones_bd):
    t, s, _ = c3.shape
    sds = jax.ShapeDtypeStruct
    return pl.pallas_call(
        _convprep_sample_kernel,
        out_shape=[sds((t, s, W), F32), sds((t, W, s), F32), sds((t, W, s), F32), sds((t, W, s), F32),
                   sds((t, 8, s), F32), sds((2, s, W), F32), sds((3, s, 3 * W), F32)],
        compiler_params=pltpu.CompilerParams(vmem_limit_bytes=VMEM_LIMIT), name="convprep_sample",
    )(c3, qkv, ba, cb0, cd0, wb, wd, gp, ones_bd)


def _gdn_sample_kernel(qt_ref, kt_ref, vt_ref, gbt_ref, s0_ref, ot_ref, s1_ref):
    h = pl.program_id(0)
    nt = qt_ref.shape[0]
    ns = qt_ref.shape[2]
    for t in range(nt):
        src = s0_ref if t == 0 else s1_ref
        b = gbt_ref[t, pl.ds(h, 1), :]
        a = jnp.exp(gbt_ref[t, pl.ds(NH + h, 1), :])

        def ks_body(i, acc):
            return acc + kt_ref[t, pl.ds(i, 1), :] * src[i]

        ks = lax.fori_loop(0, DH, ks_body, jnp.zeros((DH, ns), F32), unroll=8)
        u = b * (vt_ref[t] - a * ks)

        def up_body(i, acc):
            sn = a * src[i] + kt_ref[t, pl.ds(i, 1), :] * u
            s1_ref[i] = sn
            return acc + qt_ref[t, pl.ds(i, 1), :] * sn

        ot_ref[t] = lax.fori_loop(0, DH, up_body, jnp.zeros((DH, ns), F32), unroll=8)


def _gdn_sample(qt, kt, vt, gbt, s0):
    t, _, s = qt.shape
    hb = pl.BlockSpec((t, DH, s), lambda h: (0, h, 0))
    sb = pl.BlockSpec((None, DH, DH, s), lambda h: (h, 0, 0, 0))
    return pl.pallas_call(
        _gdn_sample_kernel, grid=(NH,),
        in_specs=[hb, hb, hb, pl.BlockSpec((t, 8, s), lambda h: (0, 0, 0)), sb],
        out_specs=[hb, sb],
        out_shape=[jax.ShapeDtypeStruct((t, W, s), F32), jax.ShapeDtypeStruct((NH, DH, DH, s), F32)],
        compiler_params=_cp("arbitrary"), name="gdn_sample",
    )(qt, kt, vt, gbt, s0)


def _gdn_post_sample_kernel(ot_ref, z_ref, nw_ref, ones_ref, y_ref):
    for t in range(ot_ref.shape[0]):
        o = ot_ref[t].T
        ms = _head_sumsq(o, ones_ref) * (1.0 / DH)
        y_ref[t] = o * lax.rsqrt(ms + RMS_EPS) * nw_ref[...] * _silu(z_ref[t])


def _gdn_post_sample(ot, z, nw4, ones_bd):
    t, _, s = ot.shape
    return pl.pallas_call(
        _gdn_post_sample_kernel, out_shape=jax.ShapeDtypeStruct((t, s, W), F32), name="gdn_post_sample",
    )(ot, z, nw4, ones_bd)


def _sb_sample_kernel(pt_ref, q_ref, kn_ref, vn_ref, *rest):
    n_pages = (len(rest) - 4) // 2
    kp = rest[:n_pages]
    vp = rest[n_pages:2 * n_pages]
    u_ref, o_ref, kpad, vpad = rest[2 * n_pages:]
    nt = q_ref.shape[0]
    nq = NH * nt
    row = lax.broadcasted_iota(jnp.int32, (nq, W), 0)
    lane = lax.broadcasted_iota(jnp.int32, (nq, W), 1)
    q = q_ref[...] * (DH ** -0.5)
    qbd = jnp.where(row // nt == lane // DH, jnp.concatenate([q] * NH, axis=0), 0.0).astype(BF16)

    kpad[...] = jnp.zeros(kpad.shape, F32)
    vpad[...] = jnp.zeros(vpad.shape, F32)
    kpad[0:nt, :] = kn_ref[...]
    vpad[0:nt, :] = vn_ref[...]
    qi = lax.broadcasted_iota(jnp.int32, (nq, PAGE), 0) % nt
    ki = lax.broadcasted_iota(jnp.int32, (nq, PAGE), 1)
    z = _dot_nt(qbd, kpad[...].astype(BF16))
    w, r = _sb_block(z, jnp.zeros((nq, 1), F32), u_ref, ki < qi)
    acc = _dot(w.astype(BF16), vpad[...].astype(BF16))

    pages = list(range(n_pages - 1, -1, -1))
    z = [_dot(qbd, kp[p][...].astype(BF16)) for p in pages]
    sp = _each(_softplus, z)
    e = _each(lambda s: _dot_exact_rhs(-s, u_ref[...]), sp)
    for p, zp, spp, ep in zip(pages, z, sp, e):
        w = jnp.exp((zp - spp) + ep + r)
        acc = acc + _dot_nt(w.astype(BF16), vp[p][...].astype(BF16))
        r = r - jnp.sum(spp, axis=-1, keepdims=True)

    sel = jnp.where(row // nt == lane // DH, acc, 0.0)
    out = sel[0:nt]
    for h in range(1, NH):
        out = out + sel[h * nt:(h + 1) * nt]
    o_ref[...] = out


def _sb_sample(page_table, q, kn, vn, cache_kt, cache_vt, layer, u_strict):
    s, t, _ = q.shape
    n_pages = page_table.shape[1]
    seq = pl.BlockSpec((None, t, W), lambda s_, pt: (s_, 0, 0))

    def page_spec(p):
        return pl.BlockSpec((None, None, W, PAGE), lambda s_, pt: (layer, pt[s_, p], 0, 0))

    grid_spec = pltpu.PrefetchScalarGridSpec(
        num_scalar_prefetch=1, grid=(s,),
        in_specs=[seq, seq, seq] + [page_spec(p) for p in range(n_pages)] * 2 +
                 [pl.BlockSpec((PAGE, PAGE), lambda s_, pt: (0, 0))],
        out_specs=seq,
        scratch_shapes=[pltpu.VMEM((PAGE, W), F32), pltpu.VMEM((PAGE, W), F32)])
    return pl.pallas_call(
        _sb_sample_kernel, grid_spec=grid_spec,
        out_shape=jax.ShapeDtypeStruct((s, t, W), F32),
        compiler_params=_cp("arbitrary"), name="sb_sample",
    )(page_table, q, kn, vn, *([cache_kt] * n_pages), *([cache_vt] * n_pages), u_strict)


def _layer_weights(l, w_in, ssm_a_re, ssm_a_im, ssm_log_dt, ssm_b_re, ssm_b_im, ssm_c_re, ssm_c_im, ssm_d, w_glu,
                   conv_b_w, delta_conv_w, delta_a_log, delta_dt_bias, delta_norm_w, w_branch, w_gate, w_o,
                   ln1_g, ln1_b, w_ffn_up, w_ffn_down, ln2_g, ln2_b):
    w = w_in[l]
    w_in_p = jnp.concatenate([w[:, :2048], jnp.pad(w[:, 2048:2056], ((0, 0), (0, 120))), w[:, 2056:]],
                             axis=1).astype(BF16)
    apow, bbd = _s5_prep(ssm_a_re[l], ssm_a_im[l], ssm_log_dt[l], ssm_b_re[l], ssm_b_im[l])
    gp = jnp.zeros((2, 128), F32)
    gp = gp.at[0, NH:2 * NH].set(delta_a_log[l]).at[1, NH:2 * NH].set(delta_dt_bias[l])
    hid = jnp.arange(W) // DH
    return dict(
        w_in=w_in_p, apow=apow, bbd=bbd.astype(BF16), cre=_s5_cmat(ssm_c_re[l]).astype(BF16),
        cim=_s5_cmat(ssm_c_im[l]).astype(BF16), d=ssm_d[l].reshape(1, W), wglu=w_glu[l].astype(BF16),
        wb=conv_b_w[l], wd=delta_conv_w[l], gp=gp, nw=delta_norm_w[l].reshape(1, DH),
        nw4=jnp.tile(delta_norm_w[l], NH).reshape(1, W),
        ones_bd=(hid[:, None] == hid[None, :]).astype(BF16),
        wbr=w_branch[l].astype(BF16), wg=w_gate[l].astype(BF16), wo=w_o[l].astype(BF16),
        ln1g=ln1_g[l].reshape(1, D), ln1b=ln1_b[l].reshape(1, D),
        wu=w_ffn_up[l].astype(BF16), wdn=w_ffn_down[l].astype(BF16),
        ln2g=ln2_g[l].reshape(1, D), ln2b=ln2_b[l].reshape(1, D))


def _prompt_layer(x, mod, p, u_strict):
    b, l, _ = x.shape
    u, c3, qkv, z, ba, q, k, v, k16, vt16 = _proj(x, mod, p["w_in"], 512, True)
    ya, hre, him = _s5_prompt(u, p["apow"], p["bbd"], p["cre"], p["cim"], p["d"], p["wglu"])
    yb, qd, kd, vd, gb, cbs, cds = _convprep_prompt(c3, qkv, ba, p["wb"], p["wd"], p["gp"], p["ones_bd"])
    uv, w, qk, qdec, kdt, egl = _gdn_local(qd, kd, vd, gb)
    yc, s_new = _gdn_scan(uv, w, qk, qdec, kdt, egl, z, p["nw"])
    idx = jnp.arange(SB_TK)
    a_incl = (idx[None, :] >= idx[:, None]).astype(BF16)
    yd = _sb_prompt(q, k16, vt16, jnp.concatenate([a_incl, a_incl], axis=1))
    x1 = _merge(x, mod, ya, yb, yc, yd, p["wg"], p["wbr"], p["wo"], p["ln1g"], p["ln1b"], 256)
    x2 = _ffn(x1, mod, p["wu"], p["wdn"], p["ln2g"], p["ln2b"], 512)
    new = (k.reshape(b, l, NH, DH), v.reshape(b, l, NH, DH), hre.reshape(b, G, P), him.reshape(b, G, P),
           cbs[:, 6:8], s_new, cds[:, 5:8])
    return x2, new


def _sample_layer(x, mod, p, st, page_table, cache_k, cache_v, layer, u_strict):
    h0re, h0im, cb0, s0, cd0 = st
    s = h0re.shape[0]
    n = x.shape[1]
    t = n // s
    u, c3, qkv, z, ba, q, k, v = _proj(x, mod, p["w_in"], n, False)
    tm = lambda a: a.reshape(t, s, a.shape[-1])
    sm = lambda a: tm(a).transpose(1, 0, 2)
    ya, hre, him = _s5_sample(tm(u), h0re.reshape(s, NS), h0im.reshape(s, NS), p["apow"], p["bbd"], p["cre"],
                              p["cim"], p["d"], p["wglu"])
    yb, qt, kt, vt, gbt, cbs, cds = _convprep_sample(tm(c3), tm(qkv), tm(ba), cb0.transpose(1, 0, 2),
                                                      cd0.transpose(1, 0, 2), p["wb"], p["wd"], p["gp"],
                                                      p["ones_bd"])
    ot, s1 = _gdn_sample(qt, kt, vt, gbt, s0.transpose(1, 2, 3, 0))
    yc = _gdn_post_sample(ot, tm(z), p["nw4"], p["ones_bd"])
    ksm, vsm = sm(k), sm(v)
    yd = _sb_sample(page_table, sm(q), ksm, vsm, cache_k, cache_v, layer, u_strict).transpose(1, 0, 2)
    flat = lambda a: a.reshape(1, n, W)
    x1 = _merge(x, mod, flat(ya), flat(yb), flat(yc), flat(yd), p["wg"], p["wbr"], p["wo"], p["ln1g"], p["ln1b"],
                min(256, n))
    x2 = _ffn(x1, mod, p["wu"], p["wdn"], p["ln2g"], p["ln2b"], n)
    new = (ksm.reshape(s, t, NH, DH), vsm.reshape(s, t, NH, DH), hre.reshape(s, G, P), him.reshape(s, G, P),
           cbs.transpose(1, 0, 2), s1.transpose(3, 0, 1, 2), cds.transpose(1, 0, 2))
    return x2, new


def kernel(x_prompt, x_sample, cache_k, cache_v, state_ssm_re, state_ssm_im, state_conv_b, state_delta,
           state_conv_delta, page_table, c_prompt, c_sample, w_ada, b_ada, w_in, ssm_a_re, ssm_a_im, ssm_log_dt,
           ssm_b_re, ssm_b_im, ssm_c_re, ssm_c_im, ssm_d, w_glu, conv_b_w, delta_conv_w, delta_a_log,
           delta_dt_bias, delta_norm_w, w_branch, w_gate, w_o, ln1_g, ln1_b, w_ffn_up, w_ffn_down, ln2_g, ln2_b):
    depth = w_ada.shape[0]
    bp, lp, _ = x_prompt.shape
    bs, ts, _ = x_sample.shape
    n_pool = cache_k.shape[1]

    c_all = jnp.concatenate([c_prompt, c_sample], axis=0)
    pad = (-c_all.shape[0]) % 8
    mod = _mod(jnp.pad(c_all, ((0, pad), (0, 0))), w_ada, b_ada)
    mod_p = mod[:, :bp].reshape(depth, bp, 6, 1, D)
    mod_s = mod[:, bp:bp + bs].reshape(depth, bs, 6, D).transpose(0, 2, 1, 3)
    mod_s = jnp.broadcast_to(mod_s[:, :, None], (depth, 6, ts, bs, D)).reshape(depth, 1, 6, ts * bs, D)

    ck = cache_k.transpose(0, 1, 3, 4, 2).reshape(depth, n_pool, W, PAGE)
    cv = cache_v.transpose(0, 1, 3, 4, 2).reshape(depth, n_pool, W, PAGE)
    u_strict = _strict_upper(PAGE)

    x_p = x_prompt
    x_s = x_sample.transpose(1, 0, 2).reshape(1, ts * bs, D)
    new_p, new_s = [], []
    for l in range(depth):
        p = _layer_weights(l, w_in, ssm_a_re, ssm_a_im, ssm_log_dt, ssm_b_re, ssm_b_im, ssm_c_re, ssm_c_im, ssm_d,
                           w_glu, conv_b_w, delta_conv_w, delta_a_log, delta_dt_bias, delta_norm_w, w_branch,
                           w_gate, w_o, ln1_g, ln1_b, w_ffn_up, w_ffn_down, ln2_g, ln2_b)
        x_p, np_ = _prompt_layer(x_p, mod_p[l], p, u_strict)
        new_p.append(np_)
        st = (state_ssm_re[l], state_ssm_im[l], state_conv_b[l], state_delta[l], state_conv_delta[l])
        x_s, ns_ = _sample_layer(x_s, mod_s[l], p, st, page_table, ck, cv, l, u_strict)
        new_s.append(ns_)
    k_p, v_p, re_p, im_p, cb_p, d_p, cd_p = [jnp.stack(t, axis=0) for t in zip(*new_p)]
    k_s, v_s, re_s, im_s, cb_s, d_s, cd_s = [jnp.stack(t, axis=0) for t in zip(*new_s)]
    y_s = x_s.reshape(ts, bs, D).transpose(1, 0, 2)
    return (x_p, y_s, k_p, v_p, k_s, v_s, re_p, im_p, re_s, im_s, cb_p, cb_s, d_p, d_s, cd_p, cd_s)
```

```python
import functools
import math

import jax
import jax.numpy as jnp
from jax import lax
from jax.experimental import pallas as pl
from jax.experimental.pallas import tpu as pltpu

F32 = jnp.float32
BF16 = jnp.bfloat16

D = 1024
W = 256
NH = 4
DH = 64
G = 16
GW = 16
P = 64
NS = G * P
DFF = 2816
CHUNK = 64
PAGE = 128
ALPHA = (2 * 2) ** 0.25
LN_EPS = 1e-5
RMS_EPS = 1e-6
LOG2E = 1.4426950408889634
VMEM_LIMIT = 56 * 1024 * 1024

SEGS = (("ssm", 256, 0, 256), ("conv", 768, 256, 768), ("qkv", 768, 1024, 768), ("z", 256, 1792, 256),
        ("ba", 128, 2048, 8), ("q", 256, 2056, 256), ("k", 256, 2312, 256), ("v", 256, 2568, 256))
NP_IN = sum(s[1] for s in SEGS)


def _cp(*sem):
    return pltpu.CompilerParams(dimension_semantics=sem, vmem_limit_bytes=VMEM_LIMIT)


def _dot(a, b):
    return jnp.dot(a, b, preferred_element_type=F32)


def _dot_nt(a, b):
    return lax.dot_general(a, b, (((1,), (1,)), ((), ())), preferred_element_type=F32)


def _split(x):
    hi = x.astype(BF16)
    lo = (x - hi.astype(F32)).astype(BF16)
    return hi, lo


def _dot_exact_rhs(x, m_bf16):
    hi, lo = _split(x)
    return _dot(hi, m_bf16) + _dot(lo, m_bf16)


def _mm3(a, b):
    ah, al = _split(a)
    bh, bl = _split(b)
    return _dot(ah, bh) + (_dot(ah, bl) + _dot(al, bh))


def _mm3_nt(a, b):
    ah, al = _split(a)
    bh, bl = _split(b)
    return _dot_nt(ah, bh) + (_dot_nt(ah, bl) + _dot_nt(al, bh))


def _ln(x):
    mu = jnp.mean(x, axis=-1, keepdims=True)
    xc = x - mu
    var = jnp.mean(xc * xc, axis=-1, keepdims=True)
    return xc * lax.rsqrt(var + LN_EPS)


def _softplus(x):
    return jnp.maximum(x, 0.0) + jnp.log1p(jnp.exp(-jnp.abs(x)))


def _silu(x):
    return x * jax.nn.sigmoid(x)


def _mod_kernel(c_ref, w_ref, b_ref, o_ref):
    s = _silu(c_ref[...]).astype(BF16)
    o_ref[0] = _dot(s, w_ref[0].astype(BF16)) + b_ref[0]


def _mod(c_all, w_ada, b_ada):
    mp = c_all.shape[0]
    depth = w_ada.shape[0]
    tn = 1024
    return pl.pallas_call(
        _mod_kernel, grid=(depth, 6 * D // tn),
        in_specs=[pl.BlockSpec((mp, D), lambda l, j: (0, 0)),
                  pl.BlockSpec((1, D, tn), lambda l, j: (l, 0, j)),
                  pl.BlockSpec((1, 1, tn), lambda l, j: (l, 0, j))],
        out_specs=pl.BlockSpec((1, mp, tn), lambda l, j: (l, 0, j)),
        out_shape=jax.ShapeDtypeStruct((depth, mp, 6 * D), F32),
        compiler_params=_cp("arbitrary", "arbitrary"), name="ada_mod",
    )(c_all, w_ada, b_ada.reshape(depth, 1, 6 * D))


def _mod_spec(mod, tm):
    if mod.shape[2] == 1:
        return pl.BlockSpec((None, 6, 1, D), lambda b, i: (b, 0, 0, 0))
    return pl.BlockSpec((None, 6, tm, D), lambda b, i: (b, 0, i, 0))


def _tok_spec(tm, c):
    return pl.BlockSpec((None, tm, c), lambda b, i: (b, i, 0))


def _full_spec(shape):
    n = len(shape)
    return pl.BlockSpec(shape, lambda b, i: (0,) * n)


def _proj_kernel(x_ref, mod_ref, w_ref, *outs):
    h = (_ln(x_ref[...]) * (1.0 + mod_ref[1]) + mod_ref[0]).astype(BF16)
    off = 0
    transposed_kv = len(outs) > len(SEGS)
    for o_ref, seg in zip(outs, SEGS):
        y = _dot(h, w_ref[:, off:off + seg[1]])
        off += seg[1]
        if transposed_kv and seg[0] in ("k", "v"):
            yt = y.T
            o_ref[...] = yt
            if seg[0] == "k":
                outs[len(SEGS)][...] = y.astype(BF16)
            else:
                outs[len(SEGS) + 1][...] = yt.astype(BF16)
        else:
            o_ref[...] = y


def _proj(x, mod, w_in_p, tm, transposed_kv):
    b, l, _ = x.shape
    out_specs = [_tok_spec(tm, s[1]) for s in SEGS]
    out_shape = [jax.ShapeDtypeStruct((b, l, s[1]), F32) for s in SEGS]
    if transposed_kv:
        tspec = pl.BlockSpec((None, W, tm), lambda b, i: (b, 0, i))
        out_specs = out_specs[:-2] + [tspec, tspec, _tok_spec(tm, W), tspec]
        out_shape = out_shape[:-2] + [jax.ShapeDtypeStruct((b, W, l), F32)] * 2 + \
            [jax.ShapeDtypeStruct((b, l, W), BF16), jax.ShapeDtypeStruct((b, W, l), BF16)]
    return pl.pallas_call(
        _proj_kernel, grid=(b, l // tm),
        in_specs=[_tok_spec(tm, D), _mod_spec(mod, tm), _full_spec((D, NP_IN))],
        out_specs=out_specs, out_shape=out_shape,
        compiler_params=_cp("arbitrary", "arbitrary"), name="in_proj",
    )(x, mod, w_in_p)


def _merge_kernel(x_ref, mod_ref, ya_ref, yb_ref, yc_ref, yd_ref, wg_ref, wb_ref, wo_ref, lg_ref, lb_ref, o_ref):
    x = x_ref[...]
    h = (_ln(x) * (1.0 + mod_ref[1]) + mod_ref[0]).astype(BF16)
    acc = None
    for n, y_ref in enumerate((ya_ref, yb_ref, yc_ref, yd_ref)):
        gate = jax.nn.sigmoid(_dot(h, wg_ref[:, n * D:(n + 1) * D]))
        br = _dot(y_ref[...].astype(BF16), wb_ref[n])
        acc = gate * br if acc is None else acc + gate * br
    mixed = _dot(acc.astype(BF16), wo_ref[...])
    r = ALPHA * x + (1.0 + mod_ref[2]) * mixed
    o_ref[...] = _ln(r) * lg_ref[...] + lb_ref[...]


def _merge(x, mod, ya, yb, yc, yd, wg, wb, wo, lg, lb, tm):
    b, l, _ = x.shape
    return pl.pallas_call(
        _merge_kernel, grid=(b, l // tm),
        in_specs=[_tok_spec(tm, D), _mod_spec(mod, tm)] + [_tok_spec(tm, W)] * 4 +
                 [_full_spec((D, 4 * D)), _full_spec((4, W, D)), _full_spec((D, D)),
                  _full_spec((1, D)), _full_spec((1, D))],
        out_specs=_tok_spec(tm, D),
        out_shape=jax.ShapeDtypeStruct((b, l, D), F32),
        compiler_params=_cp("arbitrary", "arbitrary"), name="merge",
    )(x, mod, ya, yb, yc, yd, wg, wb, wo, lg, lb)


FF_CHUNK = 256


def _ffn_kernel(x_ref, mod_ref, wu_ref, wd_ref, lg_ref, lb_ref, o_ref):
    x = x_ref[...]
    h = (_ln(x) * (1.0 + mod_ref[4]) + mod_ref[3]).astype(BF16)
    acc = None
    for c in range(DFF // FF_CHUNK):
        lo = c * FF_CHUNK
        up_a = _dot(h, wu_ref[:, lo:lo + FF_CHUNK])
        up_b = _dot(h, wu_ref[:, DFF + lo:DFF + lo + FF_CHUNK])
        t = (_silu(up_a) * up_b).astype(BF16)
        d = _dot(t, wd_ref[lo:lo + FF_CHUNK, :])
        acc = d if acc is None else acc + d
    r = ALPHA * x + (1.0 + mod_ref[5]) * acc
    o_ref[...] = _ln(r) * lg_ref[...] + lb_ref[...]


def _ffn(x, mod, wu, wd, lg, lb, tm):
    b, l, _ = x.shape
    return pl.pallas_call(
        _ffn_kernel, grid=(b, l // tm),
        in_specs=[_tok_spec(tm, D), _mod_spec(mod, tm), _full_spec((D, 2 * DFF)), _full_spec((DFF, D)),
                  _full_spec((1, D)), _full_spec((1, D))],
        out_specs=_tok_spec(tm, D),
        out_shape=jax.ShapeDtypeStruct((b, l, D), F32),
        compiler_params=_cp("arbitrary", "arbitrary"), name="ffn",
    )(x, mod, wu, wd, lg, lb)


def _s5_prep_kernel(are_ref, aim_ref, ldt_ref, bre_ref, bim_ref, apow_ref, bbre_ref, bbim_ref):
    a_re = are_ref[...]
    a_im = aim_ref[...]
    dt = jnp.exp(ldt_ref[...])
    mag = jnp.exp(dt * a_re)
    ab_re = mag * jnp.cos(dt * a_im)
    ab_im = mag * jnp.sin(dt * a_im)
    den = a_re * a_re + a_im * a_im
    f_re = ((ab_re - 1.0) * a_re + ab_im * a_im) / den
    f_im = (ab_im * a_re - (ab_re - 1.0) * a_im) / den
    b_re = bre_ref[...]
    b_im = bim_ref[...]
    bbre_ref[...] = f_re * b_re - f_im * b_im
    bbim_ref[...] = f_re * b_im + f_im * b_re
    pr, pi = ab_re, ab_im
    for k in range(4):
        apow_ref[2 * k] = pr
        apow_ref[2 * k + 1] = pi
        pr, pi = pr * pr - pi * pi, 2.0 * pr * pi


def _s5_prep(a_re, a_im, log_dt, b_re, b_im):
    rep = lambda t: jnp.repeat(t, GW, axis=0)
    ldt = jnp.broadcast_to(log_dt[:, None], (G, P))
    bt = lambda t: jnp.transpose(t, (0, 2, 1)).reshape(G * GW, P)
    apow_x, bb_re, bb_im = pl.pallas_call(
        _s5_prep_kernel,
        out_shape=[jax.ShapeDtypeStruct((8, G * GW, P), F32), jax.ShapeDtypeStruct((G * GW, P), F32),
                   jax.ShapeDtypeStruct((G * GW, P), F32)], name="s5_prep",
    )(rep(a_re), rep(a_im), rep(ldt), bt(b_re), bt(b_im))
    apow = apow_x[:, ::GW, :].reshape(8, NS)
    eye = jnp.eye(G, dtype=F32)
    bd = lambda t: jnp.einsum("ghp,gk->ghkp", t.reshape(G, GW, P), eye).reshape(W, NS)
    return apow, jnp.concatenate([bd(bb_re), bd(bb_im)], axis=1)


def _s5_cmat(c):
    eye = jnp.eye(G, dtype=F32)
    return jnp.einsum("ghp,gk->gpkh", c, eye).reshape(NS, W)


def _s5_out(hr, hi, u, cre_ref, cim_ref, d_ref, wglu_ref):
    y = _dot(hr.astype(BF16), cre_ref[...]) - _dot(hi.astype(BF16), cim_ref[...]) + d_ref[...] * u
    ys = jax.nn.gelu(y)
    return ys * jax.nn.sigmoid(_dot(ys.astype(BF16), wglu_ref[...]))


S5_TM = 128
LANES = 128


def _s5_prompt_kernel(u_ref, apow_ref, bbd_ref, cre_ref, cim_ref, d_ref, wglu_ref,
                      ya_ref, hre_ref, him_ref, sbuf, hprev, hbuf):
    tm = S5_TM
    i = pl.program_id(1)

    @pl.when(i == 0)
    def _():
        sbuf[0:8, :] = jnp.zeros((8, 2 * NS), F32)
        hprev[...] = jnp.zeros((8, 2 * NS), F32)

    u = u_ref[...]
    sbuf[8:8 + tm, :] = _dot(u.astype(BF16), bbd_ref[...])
    for c in range(NS // LANES):
        cr = slice(c * LANES, (c + 1) * LANES)
        ci = slice(NS + c * LANES, NS + (c + 1) * LANES)
        sr = sbuf[:, cr]
        si = sbuf[:, ci]
        for k, shift in enumerate((1, 2, 4)):
            ar = apow_ref[2 * k:2 * k + 1, cr]
            ai = apow_ref[2 * k + 1:2 * k + 2, cr]
            pr = pltpu.roll(sr, shift, 0)
            pi = pltpu.roll(si, shift, 0)
            sr, si = sr + (ar * pr - ai * pi), si + (ar * pi + ai * pr)
        a8r = apow_ref[6:7, cr]
        a8i = apow_ref[7:8, cr]
        hr = hprev[:, cr]
        hi = hprev[:, ci]
        for k in range(tm // 8):
            wr = sr[8 + 8 * k:16 + 8 * k]
            wi = si[8 + 8 * k:16 + 8 * k]
            hr, hi = a8r * hr - a8i * hi + wr, a8r * hi + a8i * hr + wi
            hbuf[8 * k:8 * k + 8, cr] = hr
            hbuf[8 * k:8 * k + 8, ci] = hi
        hprev[:, cr] = hr
        hprev[:, ci] = hi
    sbuf[0:8, :] = sbuf[tm:tm + 8, :]
    ya_ref[...] = _s5_out(hbuf[:, :NS], hbuf[:, NS:], u, cre_ref, cim_ref, d_ref, wglu_ref)

    @pl.when(i == pl.num_programs(1) - 1)
    def _():
        hre_ref[...] = hprev[7:8, :NS]
        him_ref[...] = hprev[7:8, NS:]


def _s5_prompt(u, apow, bbd, cre, cim, d, wglu):
    b, l, _ = u.shape
    tm = S5_TM
    return pl.pallas_call(
        _s5_prompt_kernel, grid=(b, l // tm),
        in_specs=[_tok_spec(tm, W), _full_spec((8, NS)), _full_spec((W, 2 * NS)), _full_spec((NS, W)),
                  _full_spec((NS, W)), _full_spec((1, W)), _full_spec((W, W))],
        out_specs=[_tok_spec(tm, W), pl.BlockSpec((None, 1, NS), lambda b, i: (b, 0, 0)),
                   pl.BlockSpec((None, 1, NS), lambda b, i: (b, 0, 0))],
        out_shape=[jax.ShapeDtypeStruct((b, l, W), F32), jax.ShapeDtypeStruct((b, 1, NS), F32),
                   jax.ShapeDtypeStruct((b, 1, NS), F32)],
        scratch_shapes=[pltpu.VMEM((tm + 8, 2 * NS), F32), pltpu.VMEM((8, 2 * NS), F32),
                        pltpu.VMEM((tm, 2 * NS), F32)],
        compiler_params=_cp("arbitrary", "arbitrary"), name="s5_prompt",
    )(u, apow, bbd, cre, cim, d, wglu)


def _s5_sample_kernel(u_ref, h0re_ref, h0im_ref, apow_ref, bbd_ref, cre_ref, cim_ref, d_ref, wglu_ref,
                      ya_ref, hre_ref, him_ref):
    ar = apow_ref[0:1, :]
    ai = apow_ref[1:2, :]
    hr = h0re_ref[...]
    hi = h0im_ref[...]
    for t in range(u_ref.shape[0]):
        u = u_ref[t]
        bu = _dot(u.astype(BF16), bbd_ref[...])
        hr, hi = ar * hr - ai * hi + bu[:, :NS], ar * hi + ai * hr + bu[:, NS:]
        ya_ref[t] = _s5_out(hr, hi, u, cre_ref, cim_ref, d_ref, wglu_ref)
    hre_ref[...] = hr
    him_ref[...] = hi


def _s5_sample(u, h0re, h0im, apow, bbd, cre, cim, d, wglu):
    t, s, _ = u.shape
    return pl.pallas_call(
        _s5_sample_kernel,
        out_shape=[jax.ShapeDtypeStruct((t, s, W), F32), jax.ShapeDtypeStruct((s, NS), F32),
                   jax.ShapeDtypeStruct((s, NS), F32)],
        compiler_params=pltpu.CompilerParams(vmem_limit_bytes=VMEM_LIMIT), name="s5_sample",
    )(u, h0re, h0im, apow, bbd, cre, cim, d, wglu)


def _head_sumsq(x, ones_ref):
    return _dot_exact_rhs(x * x, ones_ref[...])


def _delta_act(conv, ba, gp_ref, ones_ref):
    a = _silu(conv)
    q = a[:, :W]
    k = a[:, W:2 * W]
    v = a[:, 2 * W:]
    q = q * lax.rsqrt(_head_sumsq(q, ones_ref) + RMS_EPS) * (DH ** -0.5)
    k = k * lax.rsqrt(_head_sumsq(k, ones_ref) + RMS_EPS)
    lane = lax.broadcasted_iota(jnp.int32, ba.shape, 1)
    beta = jax.nn.sigmoid(ba)
    g = -jnp.exp(gp_ref[0:1, :]) * _softplus(ba + gp_ref[1:2, :])
    gb = jnp.where(lane < NH, beta, jnp.where(lane < 2 * NH, g, 0.0))
    return q, k, v, gb


CONV_TM = 256


def _convprep_prompt_kernel(c3_ref, qkv_ref, ba_ref, wb_ref, wd_ref, gp_ref, ones_ref,
                            yb_ref, q_ref, k_ref, v_ref, gb_ref, cbs_ref, cds_ref, cbuf, dbuf):
    tm = CONV_TM
    i = pl.program_id(1)

    @pl.when(i == 0)
    def _():
        cbuf[0:8, :] = jnp.zeros((8, W), F32)
        dbuf[0:8, :] = jnp.zeros((8, 3 * W), F32)

    c3 = c3_ref[...]
    cx = c3[:, W:2 * W] * c3[:, 2 * W:]
    cbuf[8:8 + tm, :] = cx
    full = cbuf[...]
    conv = wb_ref[2:3, :] * cx
    for j in (1, 2):
        conv = conv + wb_ref[2 - j:3 - j, :] * pltpu.roll(full, j, 0)[8:]
    yb_ref[...] = c3[:, :W] * conv
    cbs_ref[...] = cbuf[tm:tm + 8, :]
    cbuf[0:8, :] = cbuf[tm:tm + 8, :]

    x = qkv_ref[...]
    dbuf[8:8 + tm, :] = x
    full = dbuf[...]
    conv = wd_ref[3:4, :] * x
    for j in (1, 2, 3):
        conv = conv + wd_ref[3 - j:4 - j, :] * pltpu.roll(full, j, 0)[8:]
    cds_ref[...] = dbuf[tm:tm + 8, :]
    dbuf[0:8, :] = dbuf[tm:tm + 8, :]
    q, k, v, gb = _delta_act(conv, ba_ref[...], gp_ref, ones_ref)
    q_ref[...] = q
    k_ref[...] = k
    v_ref[...] = v
    gb_ref[...] = gb


def _convprep_prompt(c3, qkv, ba, wb, wd, gp, ones_bd):
    b, l, _ = c3.shape
    tm = CONV_TM
    st = lambda c: pl.BlockSpec((None, 8, c), lambda b, i: (b, 0, 0))
    return pl.pallas_call(
        _convprep_prompt_kernel, grid=(b, l // tm),
        in_specs=[_tok_spec(tm, 3 * W), _tok_spec(tm, 3 * W), _tok_spec(tm, 128), _full_spec((3, W)),
                  _full_spec((4, 3 * W)), _full_spec((2, 128)), _full_spec((W, W))],
        out_specs=[_tok_spec(tm, W)] * 4 + [_tok_spec(tm, 128), st(W), st(3 * W)],
        out_shape=[jax.ShapeDtypeStruct((b, l, W), F32)] * 4 +
                  [jax.ShapeDtypeStruct((b, l, 128), F32), jax.ShapeDtypeStruct((b, 8, W), F32),
                   jax.ShapeDtypeStruct((b, 8, 3 * W), F32)],
        scratch_shapes=[pltpu.VMEM((tm + 8, W), F32), pltpu.VMEM((tm + 8, 3 * W), F32)],
        compiler_params=_cp("arbitrary", "arbitrary"), name="convprep_prompt",
    )(c3, qkv, ba, wb, wd, gp, ones_bd)


GDN_CPG = 2


def _each(fn, *lists):
    return [fn(*args) for args in zip(*lists)]


def _tri_inv_all(ms, ii, jj):
    eye = (ii == jj).astype(F32)
    blk = (ii // 16) == (jj // 16)
    nd = _each(lambda m: jnp.where(blk, -m, 0.0), ms)
    e = _each(lambda m: jnp.where(blk, 0.0, m), ms)
    n2 = _each(lambda a: _mm3(a, a), nd)
    n4 = _each(lambda a: _mm3(a, a), n2)
    n8 = _each(lambda a: _mm3(a, a), n4)
    d = _each(lambda a, b: _mm3(eye + a, eye + b), nd, n2)
    d = _each(lambda a, b: _mm3(a, eye + b), d, n4)
    dinv = _each(lambda a, b: _mm3(a, eye + b), d, n8)
    x = _each(_mm3, dinv, e)
    x2 = _each(lambda a: _mm3(a, a), x)
    t = _each(lambda a, b: _mm3(eye - a, eye + b), x, x2)
    return _each(_mm3, t, dinv)


def _gdn_local_kernel(q_ref, k_ref, v_ref, gb_ref, uv_ref, w_ref, qk_ref, qd_ref, kdt_ref, egl_ref):
    cs = CHUNK
    ii = lax.broadcasted_iota(jnp.int32, (cs, cs), 0)
    jj = lax.broadcasted_iota(jnp.int32, (cs, cs), 1)
    tri_incl = (ii >= jj).astype(BF16)
    tri_up = (ii <= jj).astype(BF16)
    rows = [slice(c * cs, (c + 1) * cs) for c in range(GDN_CPG)]
    gbc = [gb_ref[r, :] for r in rows]
    gc_all = _each(lambda g: sum(_dot(tri_incl, p) for p in _split(g)), gbc)
    gct_all = _each(lambda g: _dot_exact_rhs(g.T, tri_up), gbc)
    chains = [(c, h) for c in range(GDN_CPG) for h in range(NH)]
    hsl = lambda h: slice(h * DH, (h + 1) * DH)
    gcol = [gc_all[c][:, NH + h:NH + h + 1] for c, h in chains]
    grow = [gct_all[c][NH + h:NH + h + 1, :] for c, h in chains]
    beta = [gbc[c][:, h:h + 1] for c, h in chains]
    qc, kc, vc = ([ref[r, :] for r in rows] for ref in (q_ref, k_ref, v_ref))
    qh = [qc[c][:, hsl(h)] for c, h in chains]
    kh = [kc[c][:, hsl(h)] for c, h in chains]
    vh = [vc[c][:, hsl(h)] for c, h in chains]
    e = _each(lambda a, b: jnp.exp(jnp.where(ii >= jj, a - b, 0.0)), gcol, grow)
    qkk = _each(lambda q, k: _mm3_nt(jnp.concatenate([q, k], axis=0), k), qh, kh)
    qkr = [t[:cs] for t in qkk]
    kk = [t[cs:] for t in qkk]
    m = _each(lambda b, a, d: b * a * jnp.where(ii > jj, d, 0.0), beta, kk, e)
    egc = _each(jnp.exp, gcol)
    rhs = _each(lambda b, v, g, k: jnp.concatenate([b * v, (b * g) * k], axis=1), beta, vh, egc, kh)
    sol = _each(_mm3, _tri_inv_all(m, ii, jj), rhs)
    glast = [g[cs - 1:cs, :] for g in gcol]
    qk = _each(lambda a, d: a * jnp.where(ii >= jj, d, 0.0), qkr, e)
    qd = _each(lambda a, g: a * g, qh, egc)
    kdt = _each(lambda k, gl, g: (k * jnp.exp(gl - g)).T, kh, glast, gcol)
    egl = _each(lambda gl: jnp.broadcast_to(jnp.exp(gl), (8, DH)), glast)
    for c in range(GDN_CPG):
        cat = lambda ts: jnp.concatenate(ts[c * NH:(c + 1) * NH], axis=1)
        uv_ref[rows[c], :] = cat([s[:, :DH] for s in sol])
        w_ref[rows[c], :] = cat([s[:, DH:] for s in sol])
        qk_ref[rows[c], :] = cat(qk)
        qd_ref[rows[c], :] = cat(qd)
        kdt_ref[rows[c], :] = cat(kdt)
        egl_ref[c] = cat(egl)


def _gdn_local(q, k, v, gb):
    b, l, _ = q.shape
    r = GDN_CPG * CHUNK
    nc = l // CHUNK
    return pl.pallas_call(
        _gdn_local_kernel, grid=(b, l // r),
        in_specs=[_tok_spec(r, W)] * 3 + [_tok_spec(r, 128)],
        out_specs=[_tok_spec(r, W)] * 5 + [pl.BlockSpec((None, GDN_CPG, 8, W), lambda b, i: (b, i, 0, 0))],
        out_shape=[jax.ShapeDtypeStruct((b, l, W), F32)] * 5 + [jax.ShapeDtypeStruct((b, nc, 8, W), F32)],
        compiler_params=_cp("arbitrary", "arbitrary"), name="gdn_local",
    )(q, k, v, gb)


def _gdn_scan_kernel(uv_ref, w_ref, qk_ref, qd_ref, kdt_ref, egl_ref, z_ref, nw_ref, y_ref, s_ref, s_acc):
    i = pl.program_id(0)
    nb = uv_ref.shape[0]

    @pl.when(i == 0)
    def _():
        s_acc[...] = jnp.zeros(s_acc.shape, F32)

    chains = [(b, h) for b in range(nb) for h in range(NH)]
    hsl = lambda h: slice(h * DH, (h + 1) * DH)
    load = lambda ref: [ref[b][:, hsl(h)] for b, h in chains]
    s = [s_acc[b, h] for b, h in chains]
    u = _each(lambda a, w, st: a - _mm3(w, st), load(uv_ref), load(w_ref), s)
    o = _each(_mm3, load(qd_ref), s)
    o = _each(lambda a, qk, ut: a + _mm3(qk, ut), o, load(qk_ref), u)
    ks = _each(_mm3, load(kdt_ref), u)
    for (b, h), st, k in zip(chains, s, ks):
        s_acc[b, h] = egl_ref[b, 0, 0:1, hsl(h)] * st + k
    zs = load(z_ref)
    o = _each(lambda a, z: a * lax.rsqrt(jnp.mean(a * a, axis=-1, keepdims=True) + RMS_EPS) * nw_ref[...] * _silu(z),
              o, zs)
    for b in range(nb):
        y_ref[b] = jnp.concatenate(o[b * NH:(b + 1) * NH], axis=1)

    @pl.when(i == pl.num_programs(0) - 1)
    def _():
        s_ref[...] = s_acc[...]


def _gdn_scan(uv, w, qk, qd, kdt, egl, z, nw):
    b, l, _ = uv.shape
    cs = CHUNK
    blk = pl.BlockSpec((b, cs, W), lambda i: (0, i, 0))
    return pl.pallas_call(
        _gdn_scan_kernel, grid=(l // cs,),
        in_specs=[blk] * 5 + [pl.BlockSpec((b, 1, 8, W), lambda i: (0, i, 0, 0)), blk,
                              pl.BlockSpec((1, DH), lambda i: (0, 0))],
        out_specs=[blk, pl.BlockSpec((b, NH, DH, DH), lambda i: (0, 0, 0, 0))],
        out_shape=[jax.ShapeDtypeStruct((b, l, W), F32), jax.ShapeDtypeStruct((b, NH, DH, DH), F32)],
        scratch_shapes=[pltpu.VMEM((b, NH, DH, DH), F32)],
        compiler_params=_cp("arbitrary"), name="gdn_scan",
    )(uv, w, qk, qd, kdt, egl, z, nw)


SB_TQ = 256
SB_TK = 128


def _sb_block(z, r, u_ref, mask):
    sp = _softplus(z)
    ln = -sp
    if mask is not None:
        ln = jnp.where(mask, ln, 0.0)
    e = _dot_exact_rhs(ln, u_ref[...])
    w = jnp.exp((z - sp) + e + r)
    if mask is not None:
        w = jnp.where(mask, w, 0.0)
    return w, r + jnp.sum(ln, axis=-1, keepdims=True)


def _sb_prompt_kernel(q_ref, k_ref, vt_ref, a_ref, o_ref, nz_buf, et_buf, wt_buf):
    i = pl.program_id(1)
    tq, tk = SB_TQ, SB_TK
    qt = (q_ref[...] * (-(DH ** -0.5) * LOG2E)).T
    rowh = lax.broadcasted_iota(jnp.int32, (W, tq), 0) // DH
    qbd = jnp.concatenate([jnp.where(rowh == h, qt, 0.0) for h in range(NH)], axis=1).astype(BF16)
    kk = lax.broadcasted_iota(jnp.int32, (tk, NH * tq), 0)
    qq = lax.broadcasted_iota(jnp.int32, (tk, NH * tq), 1) % tq

    def logits(j):
        start = pl.multiple_of(jnp.maximum(j, 0) * tk, tk)
        return _dot(k_ref[pl.ds(start, tk), :], qbd)

    def suffix(nz, mask=None):
        ln = jnp.minimum(nz, 0.0) - jnp.log2(1.0 + jnp.exp2(-jnp.abs(nz)))
        if mask is not None:
            ln = jnp.where(mask, ln, 0.0)
        return _dot(a_ref[...], ln.astype(BF16))

    def weights(nz, et, r, mask=None):
        wt = jnp.exp2((et - nz) + r)
        if mask is not None:
            wt = jnp.where(mask, wt, 0.0)
        return wt.astype(BF16)

    def attend(j, wt):
        vt = vt_ref[:, pl.ds(pl.multiple_of(j * tk, tk), tk)]
        return jnp.concatenate([_dot(vt[h * DH:(h + 1) * DH, :], wt[:, h * tq:(h + 1) * tq])
                                for h in range(NH)], axis=0)

    nslot = lambda j: lax.rem(j + 9, 3)
    eslot = lambda j: lax.rem(j + 4, 2)
    nd = tq // tk
    acc = jnp.zeros((W, tq), F32)
    r = jnp.zeros((1, NH * tq), F32)
    j0 = i * nd - 1
    jds = [(i + 1) * nd - 1 - d for d in range(nd)]
    causal = [(jd * tk + kk) < (i * tq + qq) for jd in jds]
    nzd = [logits(jd) for jd in jds]
    nz0, nz1, nz2 = logits(j0), logits(j0 - 1), logits(j0 - 2)
    etd = _each(suffix, nzd, causal)
    et0, et1 = suffix(nz0), suffix(nz1)
    for jd, nz, et, msk in zip(jds, nzd, etd, causal):
        acc = acc + attend(jd, weights(nz, et, r, msk))
        r = r + et[0:1, :]
    wt_buf[eslot(j0)] = weights(nz0, et0, r)
    r = r + et0[0:1, :]
    nz_buf[nslot(j0 - 1)] = nz1
    et_buf[eslot(j0 - 1)] = et1
    nz_buf[nslot(j0 - 2)] = nz2

    def body(t, carry):
        acc, r = carry
        j = j0 - t
        p = attend(j, wt_buf[eslot(j)])
        nzz = logits(j - 3)
        etl = suffix(nz_buf[nslot(j - 2)])
        etx = et_buf[eslot(j - 1)]
        wtx = weights(nz_buf[nslot(j - 1)], etx, r)
        et_buf[eslot(j - 2)] = etl
        nz_buf[nslot(j - 3)] = nzz
        wt_buf[eslot(j - 1)] = wtx
        return acc + p, r + etx[0:1, :]

    acc, _ = lax.fori_loop(0, i * nd, body, (acc, r))
    o_ref[...] = acc.T


def _sb_prompt(q, k16, vt16, a_strict):
    b, l, _ = q.shape
    return pl.pallas_call(
        _sb_prompt_kernel, grid=(b, l // SB_TQ),
        in_specs=[_tok_spec(SB_TQ, W),
                  pl.BlockSpec((None, l, W), lambda b, i: (b, 0, 0)),
                  pl.BlockSpec((None, W, l), lambda b, i: (b, 0, 0)),
                  _full_spec((SB_TK, SB_TK))],
        out_specs=_tok_spec(SB_TQ, W),
        out_shape=jax.ShapeDtypeStruct((b, l, W), F32),
        scratch_shapes=[pltpu.VMEM((3, SB_TK, NH * SB_TQ), F32), pltpu.VMEM((2, SB_TK, NH * SB_TQ), F32),
                        pltpu.VMEM((2, SB_TK, NH * SB_TQ), BF16)],
        compiler_params=_cp("arbitrary", "arbitrary"), name="sb_prompt",
    )(q, k16, vt16, a_strict)


def _strict_upper(n):
    i = jnp.arange(n)
    return (i[:, None] > i[None, :]).astype(BF16)


def _convprep_sample_kernel(c3_ref, qkv_ref, ba_ref, cb0_ref, cd0_ref, wb_ref, wd_ref, gp_ref, ones_ref,
                            yb_ref, qt_ref, kt_ref, vt_ref, gbt_ref, cbs_ref, cds_ref):
    nt = c3_ref.shape[0]
    xb = [cb0_ref[0], cb0_ref[1]]
    xd = [cd0_ref[0], cd0_ref[1], cd0_ref[2]]
    for t in range(nt):
        c3 = c3_ref[t]
        xb.append(c3[:, W:2 * W] * c3[:, 2 * W:])
        xd.append(qkv_ref[t])
    for t in range(nt):
        conv = xb[t] * wb_ref[0:1, :] + xb[t + 1] * wb_ref[1:2, :] + xb[t + 2] * wb_ref[2:3, :]
        yb_ref[t] = c3_ref[t][:, :W] * conv
        conv = (xd[t] * wd_ref[0:1, :] + xd[t + 1] * wd_ref[1:2, :] + xd[t + 2] * wd_ref[2:3, :]
                + xd[t + 3] * wd_ref[3:4, :])
        q, k, v, gb = _delta_act(conv, ba_ref[t], gp_ref, ones_ref)
        qt_ref[t] = q.T
        kt_ref[t] = k.T
        vt_ref[t] = v.T
        gbt_ref[t] = gb.T[0:8, :]
    cbs_ref[0] = xb[nt]
    cbs_ref[1] = xb[nt + 1]
    for j in range(3):
        cds_ref[j] = xd[nt + j]


def _convprep_sample(c3, qkv, ba, cb0, cd0, wb, wd, gp, ones_bd):
    t, s, _ = c3.shape
    sds = jax.ShapeDtypeStruct
    return pl.pallas_call(
        _convprep_sample_kernel,
        out_shape=[sds((t, s, W), F32), sds((t, W, s), F32), sds((t, W, s), F32), sds((t, W, s), F32),
                   sds((t, 8, s), F32), sds((2, s, W), F32), sds((3, s, 3 * W), F32)],
        compiler_params=pltpu.CompilerParams(vmem_limit_bytes=VMEM_LIMIT), name="convprep_sample",
    )(c3, qkv, ba, cb0, cd0, wb, wd, gp, ones_bd)


def _gdn_sample_kernel(qt_ref, kt_ref, vt_ref, gbt_ref, s0_ref, ot_ref, s1_ref):
    h = pl.program_id(0)
    nt = qt_ref.shape[0]
    ns = qt_ref.shape[2]
    for t in range(nt):
        src = s0_ref if t == 0 else s1_ref
        b = gbt_ref[t, pl.ds(h, 1), :]
        a = jnp.exp(gbt_ref[t, pl.ds(NH + h, 1), :])

        def ks_body(i, acc):
            return acc + kt_ref[t, pl.ds(i, 1), :] * src[i]

        ks = lax.fori_loop(0, DH, ks_body, jnp.zeros((DH, ns), F32), unroll=8)
        u = b * (vt_ref[t] - a * ks)

        def up_body(i, acc):
            sn = a * src[i] + kt_ref[t, pl.ds(i, 1), :] * u
            s1_ref[i] = sn
            return acc + qt_ref[t, pl.ds(i, 1), :] * sn

        ot_ref[t] = lax.fori_loop(0, DH, up_body, jnp.zeros((DH, ns), F32), unroll=8)


def _gdn_sample(qt, kt, vt, gbt, s0):
    t, _, s = qt.shape
    hb = pl.BlockSpec((t, DH, s), lambda h: (0, h, 0))
    sb = pl.BlockSpec((None, DH, DH, s), lambda h: (h, 0, 0, 0))
    return pl.pallas_call(
        _gdn_sample_kernel, grid=(NH,),
        in_specs=[hb, hb, hb, pl.BlockSpec((t, 8, s), lambda h: (0, 0, 0)), sb],
        out_specs=[hb, sb],
        out_shape=[jax.ShapeDtypeStruct((t, W, s), F32), jax.ShapeDtypeStruct((NH, DH, DH, s), F32)],
        compiler_params=_cp("arbitrary"), name="gdn_sample",
    )(qt, kt, vt, gbt, s0)


def _gdn_post_sample_kernel(ot_ref, z_ref, nw_ref, ones_ref, y_ref):
    for t in range(ot_ref.shape[0]):
        o = ot_ref[t].T
        ms = _head_sumsq(o, ones_ref) * (1.0 / DH)
        y_ref[t] = o * lax.rsqrt(ms + RMS_EPS) * nw_ref[...] * _silu(z_ref[t])


def _gdn_post_sample(ot, z, nw4, ones_bd):
    t, _, s = ot.shape
    return pl.pallas_call(
        _gdn_post_sample_kernel, out_shape=jax.ShapeDtypeStruct((t, s, W), F32), name="gdn_post_sample",
    )(ot, z, nw4, ones_bd)


def _sb_sample_kernel(pt_ref, q_ref, kn_ref, vn_ref, *rest):
    n_pages = (len(rest) - 4) // 2
    kp = rest[:n_pages]
    vp = rest[n_pages:2 * n_pages]
    u_ref, o_ref, kpad, vpad = rest[2 * n_pages:]
    nt = q_ref.shape[0]
    nq = NH * nt
    row = lax.broadcasted_iota(jnp.int32, (nq, W), 0)
    lane = lax.broadcasted_iota(jnp.int32, (nq, W), 1)
    q = q_ref[...] * (DH ** -0.5)
    qbd = jnp.where(row // nt == lane // DH, jnp.concatenate([q] * NH, axis=0), 0.0).astype(BF16)

    kpad[...] = jnp.zeros(kpad.shape, F32)
    vpad[...] = jnp.zeros(vpad.shape, F32)
    kpad[0:nt, :] = kn_ref[...]
    vpad[0:nt, :] = vn_ref[...]
    qi = lax.broadcasted_iota(jnp.int32, (nq, PAGE), 0) % nt
    ki = lax.broadcasted_iota(jnp.int32, (nq, PAGE), 1)
    z = _dot_nt(qbd, kpad[...].astype(BF16))
    w, r = _sb_block(z, jnp.zeros((nq, 1), F32), u_ref, ki < qi)
    acc = _dot(w.astype(BF16), vpad[...].astype(BF16))

    pages = list(range(n_pages - 1, -1, -1))
    z = [_dot(qbd, kp[p][...].astype(BF16)) for p in pages]
    sp = _each(_softplus, z)
    e = _each(lambda s: _dot_exact_rhs(-s, u_ref[...]), sp)
    for p, zp, spp, ep in zip(pages, z, sp, e):
        w = jnp.exp((zp - spp) + ep + r)
        acc = acc + _dot_nt(w.astype(BF16), vp[p][...].astype(BF16))
        r = r - jnp.sum(spp, axis=-1, keepdims=True)

    sel = jnp.where(row // nt == lane // DH, acc, 0.0)
    out = sel[0:nt]
    for h in range(1, NH):
        out = out + sel[h * nt:(h + 1) * nt]
    o_ref[...] = out


def _sb_sample(page_table, q, kn, vn, cache_kt, cache_vt, layer, u_strict):
    s, t, _ = q.shape
    n_pages = page_table.shape[1]
    seq = pl.BlockSpec((None, t, W), lambda s_, pt: (s_, 0, 0))

    def page_spec(p):
        return pl.BlockSpec((None, None, W, PAGE), lambda s_, pt: (layer, pt[s_, p], 0, 0))

    grid_spec = pltpu.PrefetchScalarGridSpec(
        num_scalar_prefetch=1, grid=(s,),
        in_specs=[seq, seq, seq] + [page_spec(p) for p in range(n_pages)] * 2 +
                 [pl.BlockSpec((PAGE, PAGE), lambda s_, pt: (0, 0))],
        out_specs=seq,
        scratch_shapes=[pltpu.VMEM((PAGE, W), F32), pltpu.VMEM((PAGE, W), F32)])
    return pl.pallas_call(
        _sb_sample_kernel, grid_spec=grid_spec,
        out_shape=jax.ShapeDtypeStruct((s, t, W), F32),
        compiler_params=_cp("arbitrary"), name="sb_sample",
    )(page_table, q, kn, vn, *([cache_kt] * n_pages), *([cache_vt] * n_pages), u_strict)


def _layer_weights(l, w_in, ssm_a_re, ssm_a_im, ssm_log_dt, ssm_b_re, ssm_b_im, ssm_c_re, ssm_c_im, ssm_d, w_glu,
                   conv_b_w, delta_conv_w, delta_a_log, delta_dt_bias, delta_norm_w, w_branch, w_gate, w_o,
                   ln1_g, ln1_b, w_ffn_up, w_ffn_down, ln2_g, ln2_b):
    w = w_in[l]
    w_in_p = jnp.concatenate([w[:, :2048], jnp.pad(w[:, 2048:2056], ((0, 0), (0, 120))), w[:, 2056:]],
                             axis=1).astype(BF16)
    apow, bbd = _s5_prep(ssm_a_re[l], ssm_a_im[l], ssm_log_dt[l], ssm_b_re[l], ssm_b_im[l])
    gp = jnp.zeros((2, 128), F32)
    gp = gp.at[0, NH:2 * NH].set(delta_a_log[l]).at[1, NH:2 * NH].set(delta_dt_bias[l])
    hid = jnp.arange(W) // DH
    return dict(
        w_in=w_in_p, apow=apow, bbd=bbd.astype(BF16), cre=_s5_cmat(ssm_c_re[l]).astype(BF16),
        cim=_s5_cmat(ssm_c_im[l]).astype(BF16), d=ssm_d[l].reshape(1, W), wglu=w_glu[l].astype(BF16),
        wb=conv_b_w[l], wd=delta_conv_w[l], gp=gp, nw=delta_norm_w[l].reshape(1, DH),
        nw4=jnp.tile(delta_norm_w[l], NH).reshape(1, W),
        ones_bd=(hid[:, None] == hid[None, :]).astype(BF16),
        wbr=w_branch[l].astype(BF16), wg=w_gate[l].astype(BF16), wo=w_o[l].astype(BF16),
        ln1g=ln1_g[l].reshape(1, D), ln1b=ln1_b[l].reshape(1, D),
        wu=w_ffn_up[l].astype(BF16), wdn=w_ffn_down[l].astype(BF16),
        ln2g=ln2_g[l].reshape(1, D), ln2b=ln2_b[l].reshape(1, D))


def _prompt_layer(x, mod, p, u_strict):
    b, l, _ = x.shape
    u, c3, qkv, z, ba, q, kt, vt, k16, vt16 = _proj(x, mod, p["w_in"], 512, True)
    ya, hre, him = _s5_prompt(u, p["apow"], p["bbd"], p["cre"], p["cim"], p["d"], p["wglu"])
    yb, qd, kd, vd, gb, cbs, cds = _convprep_prompt(c3, qkv, ba, p["wb"], p["wd"], p["gp"], p["ones_bd"])
    uv, w, qk, qdec, kdt, egl = _gdn_local(qd, kd, vd, gb)
    yc, s_new = _gdn_scan(uv, w, qk, qdec, kdt, egl, z, p["nw"])
    idx = jnp.arange(SB_TK)
    a_incl = (idx[None, :] >= idx[:, None]).astype(BF16)
    yd = _sb_prompt(q, k16, vt16, a_incl)
    x1 = _merge(x, mod, ya, yb, yc, yd, p["wg"], p["wbr"], p["wo"], p["ln1g"], p["ln1b"], 256)
    x2 = _ffn(x1, mod, p["wu"], p["wdn"], p["ln2g"], p["ln2b"], 512)
    unt = lambda t: t.reshape(b, NH, DH, l).transpose(0, 3, 1, 2)
    new = (unt(kt), unt(vt), hre.reshape(b, G, P), him.reshape(b, G, P),
           cbs[:, 6:8], s_new, cds[:, 5:8])
    return x2, new


def _sample_layer(x, mod, p, st, page_table, cache_k, cache_v, layer, u_strict):
    h0re, h0im, cb0, s0, cd0 = st
    s = h0re.shape[0]
    n = x.shape[1]
    t = n // s
    u, c3, qkv, z, ba, q, k, v = _proj(x, mod, p["w_in"], n, False)
    tm = lambda a: a.reshape(t, s, a.shape[-1])
    sm = lambda a: tm(a).transpose(1, 0, 2)
    ya, hre, him = _s5_sample(tm(u), h0re.reshape(s, NS), h0im.reshape(s, NS), p["apow"], p["bbd"], p["cre"],
                              p["cim"], p["d"], p["wglu"])
    yb, qt, kt, vt, gbt, cbs, cds = _convprep_sample(tm(c3), tm(qkv), tm(ba), cb0.transpose(1, 0, 2),
                                                      cd0.transpose(1, 0, 2), p["wb"], p["wd"], p["gp"],
                                                      p["ones_bd"])
    ot, s1 = _gdn_sample(qt, kt, vt, gbt, s0.transpose(1, 2, 3, 0))
    yc = _gdn_post_sample(ot, tm(z), p["nw4"], p["ones_bd"])
    ksm, vsm = sm(k), sm(v)
    yd = _sb_sample(page_table, sm(q), ksm, vsm, cache_k, cache_v, layer, u_strict).transpose(1, 0, 2)
    flat = lambda a: a.reshape(1, n, W)
    x1 = _merge(x, mod, flat(ya), flat(yb), flat(yc), flat(yd), p["wg"], p["wbr"], p["wo"], p["ln1g"], p["ln1b"],
                min(256, n))
    x2 = _ffn(x1, mod, p["wu"], p["wdn"], p["ln2g"], p["ln2b"], n)
    new = (ksm.reshape(s, t, NH, DH), vsm.reshape(s, t, NH, DH), hre.reshape(s, G, P), him.reshape(s, G, P),
           cbs.transpose(1, 0, 2), s1.transpose(3, 0, 1, 2), cds.transpose(1, 0, 2))
    return x2, new


def kernel(x_prompt, x_sample, cache_k, cache_v, state_ssm_re, state_ssm_im, state_conv_b, state_delta,
           state_conv_delta, page_table, c_prompt, c_sample, w_ada, b_ada, w_in, ssm_a_re, ssm_a_im, ssm_log_dt,
           ssm_b_re, ssm_b_im, ssm_c_re, ssm_c_im, ssm_d, w_glu, conv_b_w, delta_conv_w, delta_a_log,
           delta_dt_bias, delta_norm_w, w_branch, w_gate, w_o, ln1_g, ln1_b, w_ffn_up, w_ffn_down, ln2_g, ln2_b):
    depth = w_ada.shape[0]
    bp, lp, _ = x_prompt.shape
    bs, ts, _ = x_sample.shape
    n_pool = cache_k.shape[1]

    c_all = jnp.concatenate([c_prompt, c_sample], axis=0)
    pad = (-c_all.shape[0]) % 8
    mod = _mod(jnp.pad(c_all, ((0, pad), (0, 0))), w_ada, b_ada)
    mod_p = mod[:, :bp].reshape(depth, bp, 6, 1, D)
    mod_s = mod[:, bp:bp + bs].reshape(depth, bs, 6, D).transpose(0, 2, 1, 3)
    mod_s = jnp.broadcast_to(mod_s[:, :, None], (depth, 6, ts, bs, D)).reshape(depth, 1, 6, ts * bs, D)

    ck = cache_k.transpose(0, 1, 3, 4, 2).reshape(depth, n_pool, W, PAGE)
    cv = cache_v.transpose(0, 1, 3, 4, 2).reshape(depth, n_pool, W, PAGE)
    u_strict = _strict_upper(PAGE)

    x_p = x_prompt
    x_s = x_sample.transpose(1, 0, 2).reshape(1, ts * bs, D)
    new_p, new_s = [], []
    for l in range(depth):
        p = _layer_weights(l, w_in, ssm_a_re, ssm_a_im, ssm_log_dt, ssm_b_re, ssm_b_im, ssm_c_re, ssm_c_im, ssm_d,
                           w_glu, conv_b_w, delta_conv_w, delta_a_log, delta_dt_bias, delta_norm_w, w_branch,
                           w_gate, w_o, ln1_g, ln1_b, w_ffn_up, w_ffn_down, ln2_g, ln2_b)
        x_p, np_ = _prompt_layer(x_p, mod_p[l], p, u_strict)
        new_p.append(np_)
        st = (state_ssm_re[l], state_ssm_im[l], state_conv_b[l], state_delta[l], state_conv_delta[l])
        x_s, ns_ = _sample_layer(x_s, mod_s[l], p, st, page_table, ck, cv, l, u_strict)
        new_s.append(ns_)
    k_p, v_p, re_p, im_p, cb_p, d_p, cd_p = [jnp.stack(t, axis=0) for t in zip(*new_p)]
    k_s, v_s, re_s, im_s, cb_s, d_s, cd_s = [jnp.stack(t, axis=0) for t in zip(*new_s)]
    y_s = x_s.reshape(ts, bs, D).transpose(1, 0, 2)
    return (x_p, y_s, k_p, v_p, k_s, v_s, re_p, im_p, re_s, im_s, cb_p, cb_s, d_p, d_s, cd_p, cd_s)
```

```python
import functools
import math

import jax
import jax.numpy as jnp
from jax import lax
from jax.experimental import pallas as pl
from jax.experimental.pallas import tpu as pltpu

F32 = jnp.float32
BF16 = jnp.bfloat16

D = 1024
W = 256
NH = 4
DH = 64
G = 16
GW = 16
P = 64
NS = G * P
DFF = 2816
CHUNK = 64
PAGE = 128
ALPHA = (2 * 2) ** 0.25
LN_EPS = 1e-5
RMS_EPS = 1e-6
LOG2E = 1.4426950408889634
VMEM_LIMIT = 56 * 1024 * 1024

SEGS = (("ssm", 256, 0, 256), ("conv", 768, 256, 768), ("qkv", 768, 1024, 768), ("z", 256, 1792, 256),
        ("ba", 128, 2048, 8), ("q", 256, 2056, 256), ("k", 256, 2312, 256), ("v", 256, 2568, 256))
NP_IN = sum(s[1] for s in SEGS)


def _cp(*sem):
    return pltpu.CompilerParams(dimension_semantics=sem, vmem_limit_bytes=VMEM_LIMIT)


def _dot(a, b):
    return jnp.dot(a, b, preferred_element_type=F32)


def _dot_nt(a, b):
    return lax.dot_general(a, b, (((1,), (1,)), ((), ())), preferred_element_type=F32)


def _split(x):
    hi = x.astype(BF16)
    lo = (x - hi.astype(F32)).astype(BF16)
    return hi, lo


def _dot_exact_rhs(x, m_bf16):
    hi, lo = _split(x)
    return _dot(hi, m_bf16) + _dot(lo, m_bf16)


def _mm3(a, b):
    ah, al = _split(a)
    bh, bl = _split(b)
    return _dot(ah, bh) + (_dot(ah, bl) + _dot(al, bh))


def _mm3_nt(a, b):
    ah, al = _split(a)
    bh, bl = _split(b)
    return _dot_nt(ah, bh) + (_dot_nt(ah, bl) + _dot_nt(al, bh))


def _ln(x):
    mu = jnp.mean(x, axis=-1, keepdims=True)
    xc = x - mu
    var = jnp.mean(xc * xc, axis=-1, keepdims=True)
    return xc * lax.rsqrt(var + LN_EPS)


def _softplus(x):
    return jnp.maximum(x, 0.0) + jnp.log1p(jnp.exp(-jnp.abs(x)))


def _silu(x):
    return x * jax.nn.sigmoid(x)


def _mod_kernel(c_ref, w_ref, b_ref, o_ref):
    s = _silu(c_ref[...]).astype(BF16)
    o_ref[0] = _dot(s, w_ref[0].astype(BF16)) + b_ref[0]


def _mod(c_all, w_ada, b_ada):
    mp = c_all.shape[0]
    depth = w_ada.shape[0]
    tn = 1024
    return pl.pallas_call(
        _mod_kernel, grid=(depth, 6 * D // tn),
        in_specs=[pl.BlockSpec((mp, D), lambda l, j: (0, 0)),
                  pl.BlockSpec((1, D, tn), lambda l, j: (l, 0, j)),
                  pl.BlockSpec((1, 1, tn), lambda l, j: (l, 0, j))],
        out_specs=pl.BlockSpec((1, mp, tn), lambda l, j: (l, 0, j)),
        out_shape=jax.ShapeDtypeStruct((depth, mp, 6 * D), F32),
        compiler_params=_cp("arbitrary", "arbitrary"), name="ada_mod",
    )(c_all, w_ada, b_ada.reshape(depth, 1, 6 * D))


def _mod_spec(mod, tm):
    if mod.shape[2] == 1:
        return pl.BlockSpec((None, 6, 1, D), lambda b, i: (b, 0, 0, 0))
    return pl.BlockSpec((None, 6, tm, D), lambda b, i: (b, 0, i, 0))


def _tok_spec(tm, c):
    return pl.BlockSpec((None, tm, c), lambda b, i: (b, i, 0))


def _full_spec(shape):
    n = len(shape)
    return pl.BlockSpec(shape, lambda b, i: (0,) * n)


def _proj_kernel(x_ref, mod_ref, w_ref, *outs):
    h = (_ln(x_ref[...]) * (1.0 + mod_ref[1]) + mod_ref[0]).astype(BF16)
    off = 0
    transposed_kv = len(outs) > len(SEGS)
    for o_ref, seg in zip(outs, SEGS):
        y = _dot(h, w_ref[:, off:off + seg[1]])
        off += seg[1]
        if transposed_kv and seg[0] in ("k", "v"):
            yt = y.T
            o_ref[...] = yt
            if seg[0] == "k":
                outs[len(SEGS)][...] = y.astype(BF16)
            else:
                outs[len(SEGS) + 1][...] = yt.astype(BF16)
        else:
            o_ref[...] = y


def _proj(x, mod, w_in_p, tm, transposed_kv):
    b, l, _ = x.shape
    out_specs = [_tok_spec(tm, s[1]) for s in SEGS]
    out_shape = [jax.ShapeDtypeStruct((b, l, s[1]), F32) for s in SEGS]
    if transposed_kv:
        tspec = pl.BlockSpec((None, W, tm), lambda b, i: (b, 0, i))
        out_specs = out_specs[:-2] + [tspec, tspec, _tok_spec(tm, W), tspec]
        out_shape = out_shape[:-2] + [jax.ShapeDtypeStruct((b, W, l), F32)] * 2 + \
            [jax.ShapeDtypeStruct((b, l, W), BF16), jax.ShapeDtypeStruct((b, W, l), BF16)]
    return pl.pallas_call(
        _proj_kernel, grid=(b, l // tm),
        in_specs=[_tok_spec(tm, D), _mod_spec(mod, tm), _full_spec((D, NP_IN))],
        out_specs=out_specs, out_shape=out_shape,
        compiler_params=_cp("arbitrary", "arbitrary"), name="in_proj",
    )(x, mod, w_in_p)


def _merge_kernel(x_ref, mod_ref, ya_ref, yb_ref, yc_ref, yd_ref, wg_ref, wb_ref, wo_ref, lg_ref, lb_ref, o_ref):
    x = x_ref[...]
    h = (_ln(x) * (1.0 + mod_ref[1]) + mod_ref[0]).astype(BF16)
    acc = None
    for n, y_ref in enumerate((ya_ref, yb_ref, yc_ref, yd_ref)):
        gate = jax.nn.sigmoid(_dot(h, wg_ref[:, n * D:(n + 1) * D]))
        br = _dot(y_ref[...].astype(BF16), wb_ref[n])
        acc = gate * br if acc is None else acc + gate * br
    mixed = _dot(acc.astype(BF16), wo_ref[...])
    r = ALPHA * x + (1.0 + mod_ref[2]) * mixed
    o_ref[...] = _ln(r) * lg_ref[...] + lb_ref[...]


def _merge(x, mod, ya, yb, yc, yd, wg, wb, wo, lg, lb, tm):
    b, l, _ = x.shape
    return pl.pallas_call(
        _merge_kernel, grid=(b, l // tm),
        in_specs=[_tok_spec(tm, D), _mod_spec(mod, tm)] + [_tok_spec(tm, W)] * 4 +
                 [_full_spec((D, 4 * D)), _full_spec((4, W, D)), _full_spec((D, D)),
                  _full_spec((1, D)), _full_spec((1, D))],
        out_specs=_tok_spec(tm, D),
        out_shape=jax.ShapeDtypeStruct((b, l, D), F32),
        compiler_params=_cp("arbitrary", "arbitrary"), name="merge",
    )(x, mod, ya, yb, yc, yd, wg, wb, wo, lg, lb)


FF_CHUNK = 256


def _ffn_kernel(x_ref, mod_ref, wu_ref, wd_ref, lg_ref, lb_ref, o_ref):
    x = x_ref[...]
    h = (_ln(x) * (1.0 + mod_ref[4]) + mod_ref[3]).astype(BF16)
    acc = None
    for c in range(DFF // FF_CHUNK):
        lo = c * FF_CHUNK
        up_a = _dot(h, wu_ref[:, lo:lo + FF_CHUNK])
        up_b = _dot(h, wu_ref[:, DFF + lo:DFF + lo + FF_CHUNK])
        t = (_silu(up_a) * up_b).astype(BF16)
        d = _dot(t, wd_ref[lo:lo + FF_CHUNK, :])
        acc = d if acc is None else acc + d
    r = ALPHA * x + (1.0 + mod_ref[5]) * acc
    o_ref[...] = _ln(r) * lg_ref[...] + lb_ref[...]


def _ffn(x, mod, wu, wd, lg, lb, tm):
    b, l, _ = x.shape
    return pl.pallas_call(
        _ffn_kernel, grid=(b, l // tm),
        in_specs=[_tok_spec(tm, D), _mod_spec(mod, tm), _full_spec((D, 2 * DFF)), _full_spec((DFF, D)),
                  _full_spec((1, D)), _full_spec((1, D))],
        out_specs=_tok_spec(tm, D),
        out_shape=jax.ShapeDtypeStruct((b, l, D), F32),
        compiler_params=_cp("arbitrary", "arbitrary"), name="ffn",
    )(x, mod, wu, wd, lg, lb)


def _s5_prep_kernel(are_ref, aim_ref, ldt_ref, bre_ref, bim_ref, apow_ref, bbre_ref, bbim_ref):
    a_re = are_ref[...]
    a_im = aim_ref[...]
    dt = jnp.exp(ldt_ref[...])
    mag = jnp.exp(dt * a_re)
    ab_re = mag * jnp.cos(dt * a_im)
    ab_im = mag * jnp.sin(dt * a_im)
    den = a_re * a_re + a_im * a_im
    f_re = ((ab_re - 1.0) * a_re + ab_im * a_im) / den
    f_im = (ab_im * a_re - (ab_re - 1.0) * a_im) / den
    b_re = bre_ref[...]
    b_im = bim_ref[...]
    bbre_ref[...] = f_re * b_re - f_im * b_im
    bbim_ref[...] = f_re * b_im + f_im * b_re
    pr, pi = ab_re, ab_im
    for k in range(4):
        apow_ref[2 * k] = pr
        apow_ref[2 * k + 1] = pi
        pr, pi = pr * pr - pi * pi, 2.0 * pr * pi


def _s5_prep(a_re, a_im, log_dt, b_re, b_im):
    rep = lambda t: jnp.repeat(t, GW, axis=0)
    ldt = jnp.broadcast_to(log_dt[:, None], (G, P))
    bt = lambda t: jnp.transpose(t, (0, 2, 1)).reshape(G * GW, P)
    apow_x, bb_re, bb_im = pl.pallas_call(
        _s5_prep_kernel,
        out_shape=[jax.ShapeDtypeStruct((8, G * GW, P), F32), jax.ShapeDtypeStruct((G * GW, P), F32),
                   jax.ShapeDtypeStruct((G * GW, P), F32)], name="s5_prep",
    )(rep(a_re), rep(a_im), rep(ldt), bt(b_re), bt(b_im))
    apow = apow_x[:, ::GW, :].reshape(8, NS)
    eye = jnp.eye(G, dtype=F32)
    bd = lambda t: jnp.einsum("ghp,gk->ghkp", t.reshape(G, GW, P), eye).reshape(W, NS)
    return apow, jnp.concatenate([bd(bb_re), bd(bb_im)], axis=1)


def _s5_cmat(c):
    eye = jnp.eye(G, dtype=F32)
    return jnp.einsum("ghp,gk->gpkh", c, eye).reshape(NS, W)


def _s5_out(hr, hi, u, cre_ref, cim_ref, d_ref, wglu_ref):
    y = _dot(hr.astype(BF16), cre_ref[...]) - _dot(hi.astype(BF16), cim_ref[...]) + d_ref[...] * u
    ys = jax.nn.gelu(y)
    return ys * jax.nn.sigmoid(_dot(ys.astype(BF16), wglu_ref[...]))


S5_TM = 128
LANES = 128


def _s5_prompt_kernel(u_ref, apow_ref, bbd_ref, cre_ref, cim_ref, d_ref, wglu_ref,
                      ya_ref, hre_ref, him_ref, sbuf, hprev, hbuf):
    tm = S5_TM
    i = pl.program_id(1)

    @pl.when(i == 0)
    def _():
        sbuf[0:8, :] = jnp.zeros((8, 2 * NS), F32)
        hprev[...] = jnp.zeros((8, 2 * NS), F32)

    u = u_ref[...]
    sbuf[8:8 + tm, :] = _dot(u.astype(BF16), bbd_ref[...])
    for c in range(NS // LANES):
        cr = slice(c * LANES, (c + 1) * LANES)
        ci = slice(NS + c * LANES, NS + (c + 1) * LANES)
        sr = sbuf[:, cr]
        si = sbuf[:, ci]
        for k, shift in enumerate((1, 2, 4)):
            ar = apow_ref[2 * k:2 * k + 1, cr]
            ai = apow_ref[2 * k + 1:2 * k + 2, cr]
            pr = pltpu.roll(sr, shift, 0)
            pi = pltpu.roll(si, shift, 0)
            sr, si = sr + (ar * pr - ai * pi), si + (ar * pi + ai * pr)
        a8r = apow_ref[6:7, cr]
        a8i = apow_ref[7:8, cr]
        hr = hprev[:, cr]
        hi = hprev[:, ci]
        for k in range(tm // 8):
            wr = sr[8 + 8 * k:16 + 8 * k]
            wi = si[8 + 8 * k:16 + 8 * k]
            hr, hi = a8r * hr - a8i * hi + wr, a8r * hi + a8i * hr + wi
            hbuf[8 * k:8 * k + 8, cr] = hr
            hbuf[8 * k:8 * k + 8, ci] = hi
        hprev[:, cr] = hr
        hprev[:, ci] = hi
    sbuf[0:8, :] = sbuf[tm:tm + 8, :]
    ya_ref[...] = _s5_out(hbuf[:, :NS], hbuf[:, NS:], u, cre_ref, cim_ref, d_ref, wglu_ref)

    @pl.when(i == pl.num_programs(1) - 1)
    def _():
        hre_ref[...] = hprev[7:8, :NS]
        him_ref[...] = hprev[7:8, NS:]


def _s5_prompt(u, apow, bbd, cre, cim, d, wglu):
    b, l, _ = u.shape
    tm = S5_TM
    return pl.pallas_call(
        _s5_prompt_kernel, grid=(b, l // tm),
        in_specs=[_tok_spec(tm, W), _full_spec((8, NS)), _full_spec((W, 2 * NS)), _full_spec((NS, W)),
                  _full_spec((NS, W)), _full_spec((1, W)), _full_spec((W, W))],
        out_specs=[_tok_spec(tm, W), pl.BlockSpec((None, 1, NS), lambda b, i: (b, 0, 0)),
                   pl.BlockSpec((None, 1, NS), lambda b, i: (b, 0, 0))],
        out_shape=[jax.ShapeDtypeStruct((b, l, W), F32), jax.ShapeDtypeStruct((b, 1, NS), F32),
                   jax.ShapeDtypeStruct((b, 1, NS), F32)],
        scratch_shapes=[pltpu.VMEM((tm + 8, 2 * NS), F32), pltpu.VMEM((8, 2 * NS), F32),
                        pltpu.VMEM((tm, 2 * NS), F32)],
        compiler_params=_cp("arbitrary", "arbitrary"), name="s5_prompt",
    )(u, apow, bbd, cre, cim, d, wglu)


def _s5_sample_kernel(u_ref, h0re_ref, h0im_ref, apow_ref, bbd_ref, cre_ref, cim_ref, d_ref, wglu_ref,
                      ya_ref, hre_ref, him_ref):
    ar = apow_ref[0:1, :]
    ai = apow_ref[1:2, :]
    hr = h0re_ref[...]
    hi = h0im_ref[...]
    for t in range(u_ref.shape[0]):
        u = u_ref[t]
        bu = _dot(u.astype(BF16), bbd_ref[...])
        hr, hi = ar * hr - ai * hi + bu[:, :NS], ar * hi + ai * hr + bu[:, NS:]
        ya_ref[t] = _s5_out(hr, hi, u, cre_ref, cim_ref, d_ref, wglu_ref)
    hre_ref[...] = hr
    him_ref[...] = hi


def _s5_sample(u, h0re, h0im, apow, bbd, cre, cim, d, wglu):
    t, s, _ = u.shape
    return pl.pallas_call(
        _s5_sample_kernel,
        out_shape=[jax.ShapeDtypeStruct((t, s, W), F32), jax.ShapeDtypeStruct((s, NS), F32),
                   jax.ShapeDtypeStruct((s, NS), F32)],
        compiler_params=pltpu.CompilerParams(vmem_limit_bytes=VMEM_LIMIT), name="s5_sample",
    )(u, h0re, h0im, apow, bbd, cre, cim, d, wglu)


def _head_sumsq(x, ones_ref):
    return _dot_exact_rhs(x * x, ones_ref[...])


def _delta_act(conv, ba, gp_ref, ones_ref):
    a = _silu(conv)
    q = a[:, :W]
    k = a[:, W:2 * W]
    v = a[:, 2 * W:]
    q = q * lax.rsqrt(_head_sumsq(q, ones_ref) + RMS_EPS) * (DH ** -0.5)
    k = k * lax.rsqrt(_head_sumsq(k, ones_ref) + RMS_EPS)
    lane = lax.broadcasted_iota(jnp.int32, ba.shape, 1)
    beta = jax.nn.sigmoid(ba)
    g = -jnp.exp(gp_ref[0:1, :]) * _softplus(ba + gp_ref[1:2, :])
    gb = jnp.where(lane < NH, beta, jnp.where(lane < 2 * NH, g, 0.0))
    return q, k, v, gb


CONV_TM = 256


def _convprep_prompt_kernel(c3_ref, qkv_ref, ba_ref, wb_ref, wd_ref, gp_ref, ones_ref,
                            yb_ref, q_ref, k_ref, v_ref, gb_ref, cbs_ref, cds_ref, cbuf, dbuf):
    tm = CONV_TM
    i = pl.program_id(1)

    @pl.when(i == 0)
    def _():
        cbuf[0:8, :] = jnp.zeros((8, W), F32)
        dbuf[0:8, :] = jnp.zeros((8, 3 * W), F32)

    c3 = c3_ref[...]
    cx = c3[:, W:2 * W] * c3[:, 2 * W:]
    cbuf[8:8 + tm, :] = cx
    full = cbuf[...]
    conv = wb_ref[2:3, :] * cx
    for j in (1, 2):
        conv = conv + wb_ref[2 - j:3 - j, :] * pltpu.roll(full, j, 0)[8:]
    yb_ref[...] = c3[:, :W] * conv
    cbs_ref[...] = cbuf[tm:tm + 8, :]
    cbuf[0:8, :] = cbuf[tm:tm + 8, :]

    x = qkv_ref[...]
    dbuf[8:8 + tm, :] = x
    full = dbuf[...]
    conv = wd_ref[3:4, :] * x
    for j in (1, 2, 3):
        conv = conv + wd_ref[3 - j:4 - j, :] * pltpu.roll(full, j, 0)[8:]
    cds_ref[...] = dbuf[tm:tm + 8, :]
    dbuf[0:8, :] = dbuf[tm:tm + 8, :]
    q, k, v, gb = _delta_act(conv, ba_ref[...], gp_ref, ones_ref)
    q_ref[...] = q
    k_ref[...] = k
    v_ref[...] = v
    gb_ref[...] = gb


def _convprep_prompt(c3, qkv, ba, wb, wd, gp, ones_bd):
    b, l, _ = c3.shape
    tm = CONV_TM
    st = lambda c: pl.BlockSpec((None, 8, c), lambda b, i: (b, 0, 0))
    return pl.pallas_call(
        _convprep_prompt_kernel, grid=(b, l // tm),
        in_specs=[_tok_spec(tm, 3 * W), _tok_spec(tm, 3 * W), _tok_spec(tm, 128), _full_spec((3, W)),
                  _full_spec((4, 3 * W)), _full_spec((2, 128)), _full_spec((W, W))],
        out_specs=[_tok_spec(tm, W)] * 4 + [_tok_spec(tm, 128), st(W), st(3 * W)],
        out_shape=[jax.ShapeDtypeStruct((b, l, W), F32)] * 4 +
                  [jax.ShapeDtypeStruct((b, l, 128), F32), jax.ShapeDtypeStruct((b, 8, W), F32),
                   jax.ShapeDtypeStruct((b, 8, 3 * W), F32)],
        scratch_shapes=[pltpu.VMEM((tm + 8, W), F32), pltpu.VMEM((tm + 8, 3 * W), F32)],
        compiler_params=_cp("arbitrary", "arbitrary"), name="convprep_prompt",
    )(c3, qkv, ba, wb, wd, gp, ones_bd)


GDN_CPG = 2


def _each(fn, *lists):
    return [fn(*args) for args in zip(*lists)]


def _tri_inv_all(ms, ii, jj):
    eye = (ii == jj).astype(F32)
    blk = (ii // 16) == (jj // 16)
    nd = _each(lambda m: jnp.where(blk, -m, 0.0), ms)
    e = _each(lambda m: jnp.where(blk, 0.0, m), ms)
    n2 = _each(lambda a: _mm3(a, a), nd)
    n4 = _each(lambda a: _mm3(a, a), n2)
    n8 = _each(lambda a: _mm3(a, a), n4)
    d = _each(lambda a, b: _mm3(eye + a, eye + b), nd, n2)
    d = _each(lambda a, b: _mm3(a, eye + b), d, n4)
    dinv = _each(lambda a, b: _mm3(a, eye + b), d, n8)
    x = _each(_mm3, dinv, e)
    x2 = _each(lambda a: _mm3(a, a), x)
    t = _each(lambda a, b: _mm3(eye - a, eye + b), x, x2)
    return _each(_mm3, t, dinv)


def _gdn_local_kernel(q_ref, k_ref, v_ref, gb_ref, uv_ref, w_ref, qk_ref, qd_ref, kdt_ref, egl_ref):
    cs = CHUNK
    ii = lax.broadcasted_iota(jnp.int32, (cs, cs), 0)
    jj = lax.broadcasted_iota(jnp.int32, (cs, cs), 1)
    tri_incl = (ii >= jj).astype(BF16)
    tri_up = (ii <= jj).astype(BF16)
    rows = [slice(c * cs, (c + 1) * cs) for c in range(GDN_CPG)]
    gbc = [gb_ref[r, :] for r in rows]
    gc_all = _each(lambda g: sum(_dot(tri_incl, p) for p in _split(g)), gbc)
    gct_all = _each(lambda g: _dot_exact_rhs(g.T, tri_up), gbc)
    chains = [(c, h) for c in range(GDN_CPG) for h in range(NH)]
    hsl = lambda h: slice(h * DH, (h + 1) * DH)
    gcol = [gc_all[c][:, NH + h:NH + h + 1] for c, h in chains]
    grow = [gct_all[c][NH + h:NH + h + 1, :] for c, h in chains]
    beta = [gbc[c][:, h:h + 1] for c, h in chains]
    qc, kc, vc = ([ref[r, :] for r in rows] for ref in (q_ref, k_ref, v_ref))
    qh = [qc[c][:, hsl(h)] for c, h in chains]
    kh = [kc[c][:, hsl(h)] for c, h in chains]
    vh = [vc[c][:, hsl(h)] for c, h in chains]
    e = _each(lambda a, b: jnp.exp(jnp.where(ii >= jj, a - b, 0.0)), gcol, grow)
    qkk = _each(lambda q, k: _mm3_nt(jnp.concatenate([q, k], axis=0), k), qh, kh)
    qkr = [t[:cs] for t in qkk]
    kk = [t[cs:] for t in qkk]
    m = _each(lambda b, a, d: b * a * jnp.where(ii > jj, d, 0.0), beta, kk, e)
    egc = _each(jnp.exp, gcol)
    rhs = _each(lambda b, v, g, k: jnp.concatenate([b * v, (b * g) * k], axis=1), beta, vh, egc, kh)
    sol = _each(_mm3, _tri_inv_all(m, ii, jj), rhs)
    glast = [g[cs - 1:cs, :] for g in gcol]
    qk = _each(lambda a, d: a * jnp.where(ii >= jj, d, 0.0), qkr, e)
    qd = _each(lambda a, g: a * g, qh, egc)
    kdt = _each(lambda k, gl, g: (k * jnp.exp(gl - g)).T, kh, glast, gcol)
    egl = _each(lambda gl: jnp.broadcast_to(jnp.exp(gl), (8, DH)), glast)
    for c in range(GDN_CPG):
        cat = lambda ts: jnp.concatenate(ts[c * NH:(c + 1) * NH], axis=1)
        uv_ref[rows[c], :] = cat([s[:, :DH] for s in sol])
        w_ref[rows[c], :] = cat([s[:, DH:] for s in sol])
        qk_ref[rows[c], :] = cat(qk)
        qd_ref[rows[c], :] = cat(qd)
        kdt_ref[rows[c], :] = cat(kdt)
        egl_ref[c] = cat(egl)


def _gdn_local(q, k, v, gb):
    b, l, _ = q.shape
    r = GDN_CPG * CHUNK
    nc = l // CHUNK
    return pl.pallas_call(
        _gdn_local_kernel, grid=(b, l // r),
        in_specs=[_tok_spec(r, W)] * 3 + [_tok_spec(r, 128)],
        out_specs=[_tok_spec(r, W)] * 5 + [pl.BlockSpec((None, GDN_CPG, 8, W), lambda b, i: (b, i, 0, 0))],
        out_shape=[jax.ShapeDtypeStruct((b, l, W), F32)] * 5 + [jax.ShapeDtypeStruct((b, nc, 8, W), F32)],
        compiler_params=_cp("arbitrary", "arbitrary"), name="gdn_local",
    )(q, k, v, gb)


def _gdn_scan_kernel(uv_ref, w_ref, qk_ref, qd_ref, kdt_ref, egl_ref, z_ref, nw_ref, y_ref, s_ref, s_acc):
    i = pl.program_id(0)
    nb = uv_ref.shape[0]

    @pl.when(i == 0)
    def _():
        s_acc[...] = jnp.zeros(s_acc.shape, F32)

    chains = [(b, h) for b in range(nb) for h in range(NH)]
    hsl = lambda h: slice(h * DH, (h + 1) * DH)
    load = lambda ref: [ref[b][:, hsl(h)] for b, h in chains]
    s = [s_acc[b, h] for b, h in chains]
    u = _each(lambda a, w, st: a - _mm3(w, st), load(uv_ref), load(w_ref), s)
    dot16 = lambda a, b: _dot(a.astype(BF16), b.astype(BF16))
    o = _each(dot16, load(qd_ref), s)
    o = _each(lambda a, qk, ut: a + dot16(qk, ut), o, load(qk_ref), u)
    ks = _each(_mm3, load(kdt_ref), u)
    for (b, h), st, k in zip(chains, s, ks):
        s_acc[b, h] = egl_ref[b, 0, 0:1, hsl(h)] * st + k
    zs = load(z_ref)
    o = _each(lambda a, z: a * lax.rsqrt(jnp.mean(a * a, axis=-1, keepdims=True) + RMS_EPS) * nw_ref[...] * _silu(z),
              o, zs)
    for b in range(nb):
        y_ref[b] = jnp.concatenate(o[b * NH:(b + 1) * NH], axis=1)

    @pl.when(i == pl.num_programs(0) - 1)
    def _():
        s_ref[...] = s_acc[...]


def _gdn_scan(uv, w, qk, qd, kdt, egl, z, nw):
    b, l, _ = uv.shape
    cs = CHUNK
    blk = pl.BlockSpec((b, cs, W), lambda i: (0, i, 0))
    return pl.pallas_call(
        _gdn_scan_kernel, grid=(l // cs,),
        in_specs=[blk] * 5 + [pl.BlockSpec((b, 1, 8, W), lambda i: (0, i, 0, 0)), blk,
                              pl.BlockSpec((1, DH), lambda i: (0, 0))],
        out_specs=[blk, pl.BlockSpec((b, NH, DH, DH), lambda i: (0, 0, 0, 0))],
        out_shape=[jax.ShapeDtypeStruct((b, l, W), F32), jax.ShapeDtypeStruct((b, NH, DH, DH), F32)],
        scratch_shapes=[pltpu.VMEM((b, NH, DH, DH), F32)],
        compiler_params=_cp("arbitrary"), name="gdn_scan",
    )(uv, w, qk, qd, kdt, egl, z, nw)


SB_TQ = 256
SB_TK = 128


def _sb_block(z, r, u_ref, mask):
    sp = _softplus(z)
    ln = -sp
    if mask is not None:
        ln = jnp.where(mask, ln, 0.0)
    e = _dot_exact_rhs(ln, u_ref[...])
    w = jnp.exp((z - sp) + e + r)
    if mask is not None:
        w = jnp.where(mask, w, 0.0)
    return w, r + jnp.sum(ln, axis=-1, keepdims=True)


def _sb_prompt_kernel(q_ref, k_ref, vt_ref, a_ref, o_ref, qbd_ref, nz0, nz1, d0, d1, w0, w1, e0, e1, acc_ref):
    i = pl.program_id(1)
    tq, tk = SB_TQ, SB_TK
    nzb, db, wb, eb = (nz0, nz1), (d0, d1), (w0, w1), (e0, e1)
    tiles = [slice(h * tq, (h + 1) * tq) for h in range(NH)]
    qt = (q_ref[...] * (-(DH ** -0.5) * LOG2E)).T
    rowh = lax.broadcasted_iota(jnp.int32, (W, tq), 0) // DH
    for h, t in enumerate(tiles):
        qbd_ref[:, t] = jnp.where(rowh == h, qt, 0.0).astype(BF16)
    kk = lax.broadcasted_iota(jnp.int32, (tk, tq), 0)
    qq = lax.broadcasted_iota(jnp.int32, (tk, tq), 1)

    def stage_z(j, dst):
        kb = k_ref[pl.ds(pl.multiple_of(jnp.maximum(j, 0) * tk, tk), tk), :]
        for t in tiles:
            dst[:, t] = _dot(kb, qbd_ref[:, t])

    def stage_l(src, dst, erow, mask=None):
        for t in tiles:
            nz = src[:, t]
            ln = jnp.minimum(nz, 0.0) - jnp.log2(1.0 + jnp.exp2(-jnp.abs(nz)))
            if mask is not None:
                ln = jnp.where(mask, ln, 0.0)
            et = _dot(a_ref[...], ln.astype(BF16))
            dst[:, t] = et - nz
            erow[:, t] = et[0:1, :]

    def stage_x(src, erow, dst, r, mask=None):
        for t in tiles:
            w = jnp.exp2(src[:, t] + r[:, t])
            if mask is not None:
                w = jnp.where(mask, w, 0.0)
            dst[:, t] = w.astype(BF16)
        return r + erow[...]

    def stage_p(j, src):
        vt = vt_ref[:, pl.ds(pl.multiple_of(j * tk, tk), tk)]
        for h, t in enumerate(tiles):
            acc_ref[h * DH:(h + 1) * DH, :] += _dot(vt[h * DH:(h + 1) * DH, :], src[:, t])

    assert tq == 2 * tk
    acc_ref[...] = jnp.zeros(acc_ref.shape, F32)
    r = jnp.zeros((1, NH * tq), F32)
    jb = (2 * i + 1, 2 * i)
    causal = [(j * tk + kk) < (i * tq + qq) for j in jb]
    for s in range(2):
        stage_z(jb[s], nzb[s])
    for s in range(2):
        stage_l(nzb[s], db[s], eb[s], causal[s])
    for s in range(2):
        r = stage_x(db[s], eb[s], wb[s], r, causal[s])
    for s in range(2):
        stage_p(jb[s], wb[s])
    j0 = 2 * i - 1
    stage_z(j0, nzb[1])
    stage_z(j0 - 1, nzb[0])
    stage_l(nzb[1], db[1], eb[1])
    stage_l(nzb[0], db[0], eb[0])
    r = stage_x(db[1], eb[1], wb[1], r)
    stage_z(j0 - 2, nzb[1])

    def tick(j, p, r):
        stage_p(j, wb[p])
        stage_z(j - 3, nzb[1 - p])
        stage_l(nzb[p], db[p], eb[p])
        return stage_x(db[1 - p], eb[1 - p], wb[1 - p], r)

    def body(t, r):
        j = j0 - 2 * t
        return tick(j - 1, 0, tick(j, 1, r))

    lax.fori_loop(0, i, body, r)
    o_ref[...] = acc_ref[...].T


def _sb_prompt(q, k16, vt16, a_incl):
    b, l, _ = q.shape
    wide = (SB_TK, NH * SB_TQ)
    return pl.pallas_call(
        _sb_prompt_kernel, grid=(b, l // SB_TQ),
        in_specs=[_tok_spec(SB_TQ, W),
                  pl.BlockSpec((None, l, W), lambda b, i: (b, 0, 0)),
                  pl.BlockSpec((None, W, l), lambda b, i: (b, 0, 0)),
                  _full_spec((SB_TK, SB_TK))],
        out_specs=_tok_spec(SB_TQ, W),
        out_shape=jax.ShapeDtypeStruct((b, l, W), F32),
        scratch_shapes=[pltpu.VMEM((W, NH * SB_TQ), BF16)] + [pltpu.VMEM(wide, F32)] * 4 +
                       [pltpu.VMEM(wide, BF16)] * 2 + [pltpu.VMEM((1, NH * SB_TQ), F32)] * 2 +
                       [pltpu.VMEM((W, SB_TQ), F32)],
        compiler_params=_cp("arbitrary", "arbitrary"), name="sb_prompt",
    )(q, k16, vt16, a_incl)


def _strict_upper(n):
    i = jnp.arange(n)
    return (i[:, None] > i[None, :]).astype(BF16)


def _convprep_sample_kernel(c3_ref, qkv_ref, ba_ref, cb0_ref, cd0_ref, wb_ref, wd_ref, gp_ref, ones_ref,
                            yb_ref, qt_ref, kt_ref, vt_ref, gbt_ref, cbs_ref, cds_ref):
    nt = c3_ref.shape[0]
    xb = [cb0_ref[0], cb0_ref[1]]
    xd = [cd0_ref[0], cd0_ref[1], cd0_ref[2]]
    for t in range(nt):
        c3 = c3_ref[t]
        xb.append(c3[:, W:2 * W] * c3[:, 2 * W:])
        xd.append(qkv_ref[t])
    for t in range(nt):
        conv = xb[t] * wb_ref[0:1, :] + xb[t + 1] * wb_ref[1:2, :] + xb[t + 2] * wb_ref[2:3, :]
        yb_ref[t] = c3_ref[t][:, :W] * conv
        conv = (xd[t] * wd_ref[0:1, :] + xd[t + 1] * wd_ref[1:2, :] + xd[t + 2] * wd_ref[2:3, :]
                + xd[t + 3] * wd_ref[3:4, :])
        q, k, v, gb = _delta_act(conv, ba_ref[t], gp_ref, ones_ref)
        qt_ref[t] = q.T
        kt_ref[t] = k.T
        vt_ref[t] = v.T
        gbt_ref[t] = gb.T[0:8, :]
    cbs_ref[0] = xb[nt]
    cbs_ref[1] = xb[nt + 1]
    for j in range(3):
        cds_ref[j] = xd[nt + j]


def _convprep_sample(c3, qkv, ba, cb0, cd0, wb, wd, gp, ones_bd):
    t, s, _ = c3.shape
    sds = jax.ShapeDtypeStruct
    return pl.pallas_call(
        _convprep_sample_kernel,
        out_shape=[sds((t, s, W), F32), sds((t, W, s), F32), sds((t, W, s), F32), sds((t, W, s), F32),
                   sds((t, 8, s), F32), sds((2, s, W), F32), sds((3, s, 3 * W), F32)],
        compiler_params=pltpu.CompilerParams(vmem_limit_bytes=VMEM_LIMIT), name="convprep_sample",
    )(c3, qkv, ba, cb0, cd0, wb, wd, gp, ones_bd)


def _gdn_sample_kernel(qt_ref, kt_ref, vt_ref, gbt_ref, s0_ref, ot_ref, s1_ref):
    h = pl.program_id(0)
    nt = qt_ref.shape[0]
    ns = qt_ref.shape[2]
    for t in range(nt):
        src = s0_ref if t == 0 else s1_ref
        b = gbt_ref[t, pl.ds(h, 1), :]
        a = jnp.exp(gbt_ref[t, pl.ds(NH + h, 1), :])

        def ks_body(i, acc):
            return acc + kt_ref[t, pl.ds(i, 1), :] * src[i]

        ks = lax.fori_loop(0, DH, ks_body, jnp.zeros((DH, ns), F32), unroll=8)
        u = b * (vt_ref[t] - a * ks)

        def up_body(i, acc):
            sn = a * src[i] + kt_ref[t, pl.ds(i, 1), :] * u
            s1_ref[i] = sn
            return acc + qt_ref[t, pl.ds(i, 1), :] * sn

        ot_ref[t] = lax.fori_loop(0, DH, up_body, jnp.zeros((DH, ns), F32), unroll=8)


def _gdn_sample(qt, kt, vt, gbt, s0):
    t, _, s = qt.shape
    hb = pl.BlockSpec((t, DH, s), lambda h: (0, h, 0))
    sb = pl.BlockSpec((None, DH, DH, s), lambda h: (h, 0, 0, 0))
    return pl.pallas_call(
        _gdn_sample_kernel, grid=(NH,),
        in_specs=[hb, hb, hb, pl.BlockSpec((t, 8, s), lambda h: (0, 0, 0)), sb],
        out_specs=[hb, sb],
        out_shape=[jax.ShapeDtypeStruct((t, W, s), F32), jax.ShapeDtypeStruct((NH, DH, DH, s), F32)],
        compiler_params=_cp("arbitrary"), name="gdn_sample",
    )(qt, kt, vt, gbt, s0)


def _gdn_post_sample_kernel(ot_ref, z_ref, nw_ref, ones_ref, y_ref):
    for t in range(ot_ref.shape[0]):
        o = ot_ref[t].T
        ms = _head_sumsq(o, ones_ref) * (1.0 / DH)
        y_ref[t] = o * lax.rsqrt(ms + RMS_EPS) * nw_ref[...] * _silu(z_ref[t])


def _gdn_post_sample(ot, z, nw4, ones_bd):
    t, _, s = ot.shape
    return pl.pallas_call(
        _gdn_post_sample_kernel, out_shape=jax.ShapeDtypeStruct((t, s, W), F32), name="gdn_post_sample",
    )(ot, z, nw4, ones_bd)


def _sb_sample_kernel(pt_ref, q_ref, kn_ref, vn_ref, *rest):
    n_pages = (len(rest) - 4) // 2
    kp = rest[:n_pages]
    vp = rest[n_pages:2 * n_pages]
    u_ref, o_ref, kpad, vpad = rest[2 * n_pages:]
    nt = q_ref.shape[0]
    nq = NH * nt
    row = lax.broadcasted_iota(jnp.int32, (nq, W), 0)
    lane = lax.broadcasted_iota(jnp.int32, (nq, W), 1)
    q = q_ref[...] * (DH ** -0.5)
    qbd = jnp.where(row // nt == lane // DH, jnp.concatenate([q] * NH, axis=0), 0.0).astype(BF16)

    kpad[...] = jnp.zeros(kpad.shape, F32)
    vpad[...] = jnp.zeros(vpad.shape, F32)
    kpad[0:nt, :] = kn_ref[...]
    vpad[0:nt, :] = vn_ref[...]
    qi = lax.broadcasted_iota(jnp.int32, (nq, PAGE), 0) % nt
    ki = lax.broadcasted_iota(jnp.int32, (nq, PAGE), 1)
    z = _dot_nt(qbd, kpad[...].astype(BF16))
    w, r = _sb_block(z, jnp.zeros((nq, 1), F32), u_ref, ki < qi)
    acc = _dot(w.astype(BF16), vpad[...].astype(BF16))

    pages = list(range(n_pages - 1, -1, -1))
    z = [_dot(qbd, kp[p][...].astype(BF16)) for p in pages]
    sp = _each(_softplus, z)
    e = _each(lambda s: _dot_exact_rhs(-s, u_ref[...]), sp)
    for p, zp, spp, ep in zip(pages, z, sp, e):
        w = jnp.exp((zp - spp) + ep + r)
        acc = acc + _dot_nt(w.astype(BF16), vp[p][...].astype(BF16))
        r = r - jnp.sum(spp, axis=-1, keepdims=True)

    sel = jnp.where(row // nt == lane // DH, acc, 0.0)
    out = sel[0:nt]
    for h in range(1, NH):
        out = out + sel[h * nt:(h + 1) * nt]
    o_ref[...] = out


def _sb_sample(page_table, q, kn, vn, cache_kt, cache_vt, layer, u_strict):
    s, t, _ = q.shape
    n_pages = page_table.shape[1]
    seq = pl.BlockSpec((None, t, W), lambda s_, pt: (s_, 0, 0))

    def page_spec(p):
        return pl.BlockSpec((None, None, W, PAGE), lambda s_, pt: (layer, pt[s_, p], 0, 0))

    grid_spec = pltpu.PrefetchScalarGridSpec(
        num_scalar_prefetch=1, grid=(s,),
        in_specs=[seq, seq, seq] + [page_spec(p) for p in range(n_pages)] * 2 +
                 [pl.BlockSpec((PAGE, PAGE), lambda s_, pt: (0, 0))],
        out_specs=seq,
        scratch_shapes=[pltpu.VMEM((PAGE, W), F32), pltpu.VMEM((PAGE, W), F32)])
    return pl.pallas_call(
        _sb_sample_kernel, grid_spec=grid_spec,
        out_shape=jax.ShapeDtypeStruct((s, t, W), F32),
        compiler_params=_cp("arbitrary"), name="sb_sample",
    )(page_table, q, kn, vn, *([cache_kt] * n_pages), *([cache_vt] * n_pages), u_strict)


def _layer_weights(l, w_in, ssm_a_re, ssm_a_im, ssm_log_dt, ssm_b_re, ssm_b_im, ssm_c_re, ssm_c_im, ssm_d, w_glu,
                   conv_b_w, delta_conv_w, delta_a_log, delta_dt_bias, delta_norm_w, w_branch, w_gate, w_o,
                   ln1_g, ln1_b, w_ffn_up, w_ffn_down, ln2_g, ln2_b):
    w = w_in[l]
    w_in_p = jnp.concatenate([w[:, :2048], jnp.pad(w[:, 2048:2056], ((0, 0), (0, 120))), w[:, 2056:]],
                             axis=1).astype(BF16)
    apow, bbd = _s5_prep(ssm_a_re[l], ssm_a_im[l], ssm_log_dt[l], ssm_b_re[l], ssm_b_im[l])
    gp = jnp.zeros((2, 128), F32)
    gp = gp.at[0, NH:2 * NH].set(delta_a_log[l]).at[1, NH:2 * NH].set(delta_dt_bias[l])
    hid = jnp.arange(W) // DH
    return dict(
        w_in=w_in_p, apow=apow, bbd=bbd.astype(BF16), cre=_s5_cmat(ssm_c_re[l]).astype(BF16),
        cim=_s5_cmat(ssm_c_im[l]).astype(BF16), d=ssm_d[l].reshape(1, W), wglu=w_glu[l].astype(BF16),
        wb=conv_b_w[l], wd=delta_conv_w[l], gp=gp, nw=delta_norm_w[l].reshape(1, DH),
        nw4=jnp.tile(delta_norm_w[l], NH).reshape(1, W),
        ones_bd=(hid[:, None] == hid[None, :]).astype(BF16),
        wbr=w_branch[l].astype(BF16), wg=w_gate[l].astype(BF16), wo=w_o[l].astype(BF16),
        ln1g=ln1_g[l].reshape(1, D), ln1b=ln1_b[l].reshape(1, D),
        wu=w_ffn_up[l].astype(BF16), wdn=w_ffn_down[l].astype(BF16),
        ln2g=ln2_g[l].reshape(1, D), ln2b=ln2_b[l].reshape(1, D))


def _prompt_layer(x, mod, p, u_strict):
    b, l, _ = x.shape
    u, c3, qkv, z, ba, q, kt, vt, k16, vt16 = _proj(x, mod, p["w_in"], 512, True)
    ya, hre, him = _s5_prompt(u, p["apow"], p["bbd"], p["cre"], p["cim"], p["d"], p["wglu"])
    yb, qd, kd, vd, gb, cbs, cds = _convprep_prompt(c3, qkv, ba, p["wb"], p["wd"], p["gp"], p["ones_bd"])
    uv, w, qk, qdec, kdt, egl = _gdn_local(qd, kd, vd, gb)
    yc, s_new = _gdn_scan(uv, w, qk, qdec, kdt, egl, z, p["nw"])
    idx = jnp.arange(SB_TK)
    a_incl = (idx[None, :] >= idx[:, None]).astype(BF16)
    yd = _sb_prompt(q, k16, vt16, a_incl)
    x1 = _merge(x, mod, ya, yb, yc, yd, p["wg"], p["wbr"], p["wo"], p["ln1g"], p["ln1b"], 256)
    x2 = _ffn(x1, mod, p["wu"], p["wdn"], p["ln2g"], p["ln2b"], 512)
    unt = lambda t: t.reshape(b, NH, DH, l).transpose(0, 3, 1, 2)
    new = (unt(kt), unt(vt), hre.reshape(b, G, P), him.reshape(b, G, P),
           cbs[:, 6:8], s_new, cds[:, 5:8])
    return x2, new


def _sample_layer(x, mod, p, st, page_table, cache_k, cache_v, layer, u_strict):
    h0re, h0im, cb0, s0, cd0 = st
    s = h0re.shape[0]
    n = x.shape[1]
    t = n // s
    u, c3, qkv, z, ba, q, k, v = _proj(x, mod, p["w_in"], n, False)
    tm = lambda a: a.reshape(t, s, a.shape[-1])
    sm = lambda a: tm(a).transpose(1, 0, 2)
    ya, hre, him = _s5_sample(tm(u), h0re.reshape(s, NS), h0im.reshape(s, NS), p["apow"], p["bbd"], p["cre"],
                              p["cim"], p["d"], p["wglu"])
    yb, qt, kt, vt, gbt, cbs, cds = _convprep_sample(tm(c3), tm(qkv), tm(ba), cb0.transpose(1, 0, 2),
                                                      cd0.transpose(1, 0, 2), p["wb"], p["wd"], p["gp"],
                                                      p["ones_bd"])
    ot, s1 = _gdn_sample(qt, kt, vt, gbt, s0.transpose(1, 2, 3, 0))
    yc = _gdn_post_sample(ot, tm(z), p["nw4"], p["ones_bd"])
    ksm, vsm = sm(k), sm(v)
    yd = _sb_sample(page_table, sm(q), ksm, vsm, cache_k, cache_v, layer, u_strict).transpose(1, 0, 2)
    flat = lambda a: a.reshape(1, n, W)
    x1 = _merge(x, mod, flat(ya), flat(yb), flat(yc), flat(yd), p["wg"], p["wbr"], p["wo"], p["ln1g"], p["ln1b"],
                min(256, n))
    x2 = _ffn(x1, mod, p["wu"], p["wdn"], p["ln2g"], p["ln2b"], n)
    new = (ksm.reshape(s, t, NH, DH), vsm.reshape(s, t, NH, DH), hre.reshape(s, G, P), him.reshape(s, G, P),
           cbs.transpose(1, 0, 2), s1.transpose(3, 0, 1, 2), cds.transpose(1, 0, 2))
    return x2, new


def kernel(x_prompt, x_sample, cache_k, cache_v, state_ssm_re, state_ssm_im, state_conv_b, state_delta,
           state_conv_delta, page_table, c_prompt, c_sample, w_ada, b_ada, w_in, ssm_a_re, ssm_a_im, ssm_log_dt,
           ssm_b_re, ssm_b_im, ssm_c_re, ssm_c_im, ssm_d, w_glu, conv_b_w, delta_conv_w, delta_a_log,
           delta_dt_bias, delta_norm_w, w_branch, w_gate, w_o, ln1_g, ln1_b, w_ffn_up, w_ffn_down, ln2_g, ln2_b):
    depth = w_ada.shape[0]
    bp, lp, _ = x_prompt.shape
    bs, ts, _ = x_sample.shape
    n_pool = cache_k.shape[1]

    c_all = jnp.concatenate([c_prompt, c_sample], axis=0)
    pad = (-c_all.shape[0]) % 8
    mod = _mod(jnp.pad(c_all, ((0, pad), (0, 0))), w_ada, b_ada)
    mod_p = mod[:, :bp].reshape(depth, bp, 6, 1, D)
    mod_s = mod[:, bp:bp + bs].reshape(depth, bs, 6, D).transpose(0, 2, 1, 3)
    mod_s = jnp.broadcast_to(mod_s[:, :, None], (depth, 6, ts, bs, D)).reshape(depth, 1, 6, ts * bs, D)

    ck = cache_k.transpose(0, 1, 3, 4, 2).reshape(depth, n_pool, W, PAGE)
    cv = cache_v.transpose(0, 1, 3, 4, 2).reshape(depth, n_pool, W, PAGE)
    u_strict = _strict_upper(PAGE)

    x_p = x_prompt
    x_s = x_sample.transpose(1, 0, 2).reshape(1, ts * bs, D)
    new_p, new_s = [], []
    for l in range(depth):
        p = _layer_weights(l, w_in, ssm_a_re, ssm_a_im, ssm_log_dt, ssm_b_re, ssm_b_im, ssm_c_re, ssm_c_im, ssm_d,
                           w_glu, conv_b_w, delta_conv_w, delta_a_log, delta_dt_bias, delta_norm_w, w_branch,
                           w_gate, w_o, ln1_g, ln1_b, w_ffn_up, w_ffn_down, ln2_g, ln2_b)
        x_p, np_ = _prompt_layer(x_p, mod_p[l], p, u_strict)
        new_p.append(np_)
        st = (state_ssm_re[l], state_ssm_im[l], state_conv_b[l], state_delta[l], state_conv_delta[l])
        x_s, ns_ = _sample_layer(x_s, mod_s[l], p, st, page_table, ck, cv, l, u_strict)
        new_s.append(ns_)
    k_p, v_p, re_p, im_p, cb_p, d_p, cd_p = [jnp.stack(t, axis=0) for t in zip(*new_p)]
    k_s, v_s, re_s, im_s, cb_s, d_s, cd_s = [jnp.stack(t, axis=0) for t in zip(*new_s)]
    y_s = x_s.reshape(ts, bs, D).transpose(1, 0, 2)
    return (x_p, y_s, k_p, v_p, k_s, v_s, re_p, im_p, re_s, im_s, cb_p, cb_s, d_p, d_s, cd_p, cd_s)
```

```python
import functools
import math

import jax
import jax.numpy as jnp
from jax import lax
from jax.experimental import pallas as pl
from jax.experimental.pallas import tpu as pltpu

F32 = jnp.float32
BF16 = jnp.bfloat16

D = 1024
W = 256
NH = 4
DH = 64
G = 16
GW = 16
P = 64
NS = G * P
DFF = 2816
CHUNK = 64
PAGE = 128
ALPHA = (2 * 2) ** 0.25
LN_EPS = 1e-5
RMS_EPS = 1e-6
LOG2E = 1.4426950408889634
VMEM_LIMIT = 56 * 1024 * 1024

SEGS = (("ssm", 256, 0, 256), ("conv", 768, 256, 768), ("qkv", 768, 1024, 768), ("z", 256, 1792, 256),
        ("ba", 128, 2048, 8), ("q", 256, 2056, 256), ("k", 256, 2312, 256), ("v", 256, 2568, 256))
NP_IN = sum(s[1] for s in SEGS)


def _cp(*sem):
    return pltpu.CompilerParams(dimension_semantics=sem, vmem_limit_bytes=VMEM_LIMIT)


def _dot(a, b):
    return jnp.dot(a, b, preferred_element_type=F32)


def _dot_nt(a, b):
    return lax.dot_general(a, b, (((1,), (1,)), ((), ())), preferred_element_type=F32)


def _split(x):
    hi = x.astype(BF16)
    lo = (x - hi.astype(F32)).astype(BF16)
    return hi, lo


def _dot_exact_rhs(x, m_bf16):
    hi, lo = _split(x)
    return _dot(hi, m_bf16) + _dot(lo, m_bf16)


def _mm3(a, b):
    ah, al = _split(a)
    bh, bl = _split(b)
    return _dot(ah, bh) + (_dot(ah, bl) + _dot(al, bh))


def _mm3_nt(a, b):
    ah, al = _split(a)
    bh, bl = _split(b)
    return _dot_nt(ah, bh) + (_dot_nt(ah, bl) + _dot_nt(al, bh))


def _ln(x):
    mu = jnp.mean(x, axis=-1, keepdims=True)
    xc = x - mu
    var = jnp.mean(xc * xc, axis=-1, keepdims=True)
    return xc * lax.rsqrt(var + LN_EPS)


def _softplus(x):
    return jnp.maximum(x, 0.0) + jnp.log1p(jnp.exp(-jnp.abs(x)))


def _silu(x):
    return x * jax.nn.sigmoid(x)


def _mod_kernel(c_ref, w_ref, b_ref, o_ref):
    s = _silu(c_ref[...]).astype(BF16)
    o_ref[0] = _dot(s, w_ref[0].astype(BF16)) + b_ref[0]


def _mod(c_all, w_ada, b_ada):
    mp = c_all.shape[0]
    depth = w_ada.shape[0]
    tn = 1024
    return pl.pallas_call(
        _mod_kernel, grid=(depth, 6 * D // tn),
        in_specs=[pl.BlockSpec((mp, D), lambda l, j: (0, 0)),
                  pl.BlockSpec((1, D, tn), lambda l, j: (l, 0, j)),
                  pl.BlockSpec((1, 1, tn), lambda l, j: (l, 0, j))],
        out_specs=pl.BlockSpec((1, mp, tn), lambda l, j: (l, 0, j)),
        out_shape=jax.ShapeDtypeStruct((depth, mp, 6 * D), F32),
        compiler_params=_cp("arbitrary", "arbitrary"), name="ada_mod",
    )(c_all, w_ada, b_ada.reshape(depth, 1, 6 * D))


def _mod_spec(mod, tm):
    if mod.shape[2] == 1:
        return pl.BlockSpec((None, 6, 1, D), lambda b, i: (b, 0, 0, 0))
    return pl.BlockSpec((None, 6, tm, D), lambda b, i: (b, 0, i, 0))


def _tok_spec(tm, c):
    return pl.BlockSpec((None, tm, c), lambda b, i: (b, i, 0))


def _full_spec(shape):
    n = len(shape)
    return pl.BlockSpec(shape, lambda b, i: (0,) * n)


def _proj_kernel(x_ref, mod_ref, w_ref, *outs):
    h = (_ln(x_ref[...]) * (1.0 + mod_ref[1]) + mod_ref[0]).astype(BF16)
    off = 0
    transposed_kv = len(outs) > len(SEGS)
    for o_ref, seg in zip(outs, SEGS):
        y = _dot(h, w_ref[:, off:off + seg[1]])
        off += seg[1]
        if transposed_kv and seg[0] in ("k", "v"):
            yt = y.T
            o_ref[...] = yt
            if seg[0] == "k":
                outs[len(SEGS)][...] = y.astype(BF16)
            else:
                outs[len(SEGS) + 1][...] = yt.astype(BF16)
        else:
            o_ref[...] = y


def _proj(x, mod, w_in_p, tm, transposed_kv):
    b, l, _ = x.shape
    out_specs = [_tok_spec(tm, s[1]) for s in SEGS]
    out_shape = [jax.ShapeDtypeStruct((b, l, s[1]), F32) for s in SEGS]
    if transposed_kv:
        tspec = pl.BlockSpec((None, W, tm), lambda b, i: (b, 0, i))
        out_specs = out_specs[:-2] + [tspec, tspec, _tok_spec(tm, W), tspec]
        out_shape = out_shape[:-2] + [jax.ShapeDtypeStruct((b, W, l), F32)] * 2 + \
            [jax.ShapeDtypeStruct((b, l, W), BF16), jax.ShapeDtypeStruct((b, W, l), BF16)]
    return pl.pallas_call(
        _proj_kernel, grid=(b, l // tm),
        in_specs=[_tok_spec(tm, D), _mod_spec(mod, tm), _full_spec((D, NP_IN))],
        out_specs=out_specs, out_shape=out_shape,
        compiler_params=_cp("arbitrary", "arbitrary"), name="in_proj",
    )(x, mod, w_in_p)


def _merge_kernel(x_ref, mod_ref, ya_ref, yb_ref, yc_ref, yd_ref, wg_ref, wb_ref, wo_ref, lg_ref, lb_ref, o_ref):
    x = x_ref[...]
    h = (_ln(x) * (1.0 + mod_ref[1]) + mod_ref[0]).astype(BF16)
    acc = None
    for n, y_ref in enumerate((ya_ref, yb_ref, yc_ref, yd_ref)):
        gate = jax.nn.sigmoid(_dot(h, wg_ref[:, n * D:(n + 1) * D]))
        br = _dot(y_ref[...].astype(BF16), wb_ref[n])
        acc = gate * br if acc is None else acc + gate * br
    mixed = _dot(acc.astype(BF16), wo_ref[...])
    r = ALPHA * x + (1.0 + mod_ref[2]) * mixed
    o_ref[...] = _ln(r) * lg_ref[...] + lb_ref[...]


def _merge(x, mod, ya, yb, yc, yd, wg, wb, wo, lg, lb, tm):
    b, l, _ = x.shape
    return pl.pallas_call(
        _merge_kernel, grid=(b, l // tm),
        in_specs=[_tok_spec(tm, D), _mod_spec(mod, tm)] + [_tok_spec(tm, W)] * 4 +
                 [_full_spec((D, 4 * D)), _full_spec((4, W, D)), _full_spec((D, D)),
                  _full_spec((1, D)), _full_spec((1, D))],
        out_specs=_tok_spec(tm, D),
        out_shape=jax.ShapeDtypeStruct((b, l, D), F32),
        compiler_params=_cp("arbitrary", "arbitrary"), name="merge",
    )(x, mod, ya, yb, yc, yd, wg, wb, wo, lg, lb)


FF_CHUNK = 256


def _ffn_kernel(x_ref, mod_ref, wu_ref, wd_ref, lg_ref, lb_ref, o_ref):
    x = x_ref[...]
    h = (_ln(x) * (1.0 + mod_ref[4]) + mod_ref[3]).astype(BF16)
    acc = None
    for c in range(DFF // FF_CHUNK):
        lo = c * FF_CHUNK
        up_a = _dot(h, wu_ref[:, lo:lo + FF_CHUNK])
        up_b = _dot(h, wu_ref[:, DFF + lo:DFF + lo + FF_CHUNK])
        t = (_silu(up_a) * up_b).astype(BF16)
        d = _dot(t, wd_ref[lo:lo + FF_CHUNK, :])
        acc = d if acc is None else acc + d
    r = ALPHA * x + (1.0 + mod_ref[5]) * acc
    o_ref[...] = _ln(r) * lg_ref[...] + lb_ref[...]


def _ffn(x, mod, wu, wd, lg, lb, tm):
    b, l, _ = x.shape
    return pl.pallas_call(
        _ffn_kernel, grid=(b, l // tm),
        in_specs=[_tok_spec(tm, D), _mod_spec(mod, tm), _full_spec((D, 2 * DFF)), _full_spec((DFF, D)),
                  _full_spec((1, D)), _full_spec((1, D))],
        out_specs=_tok_spec(tm, D),
        out_shape=jax.ShapeDtypeStruct((b, l, D), F32),
        compiler_params=_cp("arbitrary", "arbitrary"), name="ffn",
    )(x, mod, wu, wd, lg, lb)


def _s5_prep_kernel(are_ref, aim_ref, ldt_ref, bre_ref, bim_ref, apow_ref, bbre_ref, bbim_ref):
    a_re = are_ref[...]
    a_im = aim_ref[...]
    dt = jnp.exp(ldt_ref[...])
    mag = jnp.exp(dt * a_re)
    ab_re = mag * jnp.cos(dt * a_im)
    ab_im = mag * jnp.sin(dt * a_im)
    den = a_re * a_re + a_im * a_im
    f_re = ((ab_re - 1.0) * a_re + ab_im * a_im) / den
    f_im = (ab_im * a_re - (ab_re - 1.0) * a_im) / den
    b_re = bre_ref[...]
    b_im = bim_ref[...]
    bbre_ref[...] = f_re * b_re - f_im * b_im
    bbim_ref[...] = f_re * b_im + f_im * b_re
    pr, pi = ab_re, ab_im
    for k in range(4):
        apow_ref[2 * k] = pr
        apow_ref[2 * k + 1] = pi
        pr, pi = pr * pr - pi * pi, 2.0 * pr * pi


def _s5_prep(a_re, a_im, log_dt, b_re, b_im):
    rep = lambda t: jnp.repeat(t, GW, axis=0)
    ldt = jnp.broadcast_to(log_dt[:, None], (G, P))
    bt = lambda t: jnp.transpose(t, (0, 2, 1)).reshape(G * GW, P)
    apow_x, bb_re, bb_im = pl.pallas_call(
        _s5_prep_kernel,
        out_shape=[jax.ShapeDtypeStruct((8, G * GW, P), F32), jax.ShapeDtypeStruct((G * GW, P), F32),
                   jax.ShapeDtypeStruct((G * GW, P), F32)], name="s5_prep",
    )(rep(a_re), rep(a_im), rep(ldt), bt(b_re), bt(b_im))
    apow = apow_x[:, ::GW, :].reshape(8, NS)
    eye = jnp.eye(G, dtype=F32)
    bd = lambda t: jnp.einsum("ghp,gk->ghkp", t.reshape(G, GW, P), eye).reshape(W, NS)
    return apow, jnp.concatenate([bd(bb_re), bd(bb_im)], axis=1)


def _s5_cmat(c):
    eye = jnp.eye(G, dtype=F32)
    return jnp.einsum("ghp,gk->gpkh", c, eye).reshape(NS, W)


def _s5_out(hr, hi, u, cre_ref, cim_ref, d_ref, wglu_ref):
    y = _dot(hr.astype(BF16), cre_ref[...]) - _dot(hi.astype(BF16), cim_ref[...]) + d_ref[...] * u
    ys = jax.nn.gelu(y)
    return ys * jax.nn.sigmoid(_dot(ys.astype(BF16), wglu_ref[...]))


S5_TM = 128
LANES = 128


def _s5_prompt_kernel(u_ref, apow_ref, bbd_ref, cre_ref, cim_ref, d_ref, wglu_ref,
                      ya_ref, hre_ref, him_ref, sbuf, hprev, hbuf):
    tm = S5_TM
    i = pl.program_id(1)

    @pl.when(i == 0)
    def _():
        sbuf[0:8, :] = jnp.zeros((8, 2 * NS), F32)
        hprev[...] = jnp.zeros((8, 2 * NS), F32)

    u = u_ref[...]
    sbuf[8:8 + tm, :] = _dot(u.astype(BF16), bbd_ref[...])
    for c in range(NS // LANES):
        cr = slice(c * LANES, (c + 1) * LANES)
        ci = slice(NS + c * LANES, NS + (c + 1) * LANES)
        sr = sbuf[:, cr]
        si = sbuf[:, ci]
        for k, shift in enumerate((1, 2, 4)):
            ar = apow_ref[2 * k:2 * k + 1, cr]
            ai = apow_ref[2 * k + 1:2 * k + 2, cr]
            pr = pltpu.roll(sr, shift, 0)
            pi = pltpu.roll(si, shift, 0)
            sr, si = sr + (ar * pr - ai * pi), si + (ar * pi + ai * pr)
        a8r = apow_ref[6:7, cr]
        a8i = apow_ref[7:8, cr]
        hr = hprev[:, cr]
        hi = hprev[:, ci]
        for k in range(tm // 8):
            wr = sr[8 + 8 * k:16 + 8 * k]
            wi = si[8 + 8 * k:16 + 8 * k]
            hr, hi = a8r * hr - a8i * hi + wr, a8r * hi + a8i * hr + wi
            hbuf[8 * k:8 * k + 8, cr] = hr
            hbuf[8 * k:8 * k + 8, ci] = hi
        hprev[:, cr] = hr
        hprev[:, ci] = hi
    sbuf[0:8, :] = sbuf[tm:tm + 8, :]
    ya_ref[...] = _s5_out(hbuf[:, :NS], hbuf[:, NS:], u, cre_ref, cim_ref, d_ref, wglu_ref)

    @pl.when(i == pl.num_programs(1) - 1)
    def _():
        hre_ref[...] = hprev[7:8, :NS]
        him_ref[...] = hprev[7:8, NS:]


def _s5_prompt(u, apow, bbd, cre, cim, d, wglu):
    b, l, _ = u.shape
    tm = S5_TM
    return pl.pallas_call(
        _s5_prompt_kernel, grid=(b, l // tm),
        in_specs=[_tok_spec(tm, W), _full_spec((8, NS)), _full_spec((W, 2 * NS)), _full_spec((NS, W)),
                  _full_spec((NS, W)), _full_spec((1, W)), _full_spec((W, W))],
        out_specs=[_tok_spec(tm, W), pl.BlockSpec((None, 1, NS), lambda b, i: (b, 0, 0)),
                   pl.BlockSpec((None, 1, NS), lambda b, i: (b, 0, 0))],
        out_shape=[jax.ShapeDtypeStruct((b, l, W), F32), jax.ShapeDtypeStruct((b, 1, NS), F32),
                   jax.ShapeDtypeStruct((b, 1, NS), F32)],
        scratch_shapes=[pltpu.VMEM((tm + 8, 2 * NS), F32), pltpu.VMEM((8, 2 * NS), F32),
                        pltpu.VMEM((tm, 2 * NS), F32)],
        compiler_params=_cp("arbitrary", "arbitrary"), name="s5_prompt",
    )(u, apow, bbd, cre, cim, d, wglu)


def _s5_sample_kernel(u_ref, h0re_ref, h0im_ref, apow_ref, bbd_ref, cre_ref, cim_ref, d_ref, wglu_ref,
                      ya_ref, hre_ref, him_ref):
    ar = apow_ref[0:1, :]
    ai = apow_ref[1:2, :]
    hr = h0re_ref[...]
    hi = h0im_ref[...]
    for t in range(u_ref.shape[0]):
        u = u_ref[t]
        bu = _dot(u.astype(BF16), bbd_ref[...])
        hr, hi = ar * hr - ai * hi + bu[:, :NS], ar * hi + ai * hr + bu[:, NS:]
        ya_ref[t] = _s5_out(hr, hi, u, cre_ref, cim_ref, d_ref, wglu_ref)
    hre_ref[...] = hr
    him_ref[...] = hi


def _s5_sample(u, h0re, h0im, apow, bbd, cre, cim, d, wglu):
    t, s, _ = u.shape
    return pl.pallas_call(
        _s5_sample_kernel,
        out_shape=[jax.ShapeDtypeStruct((t, s, W), F32), jax.ShapeDtypeStruct((s, NS), F32),
                   jax.ShapeDtypeStruct((s, NS), F32)],
        compiler_params=pltpu.CompilerParams(vmem_limit_bytes=VMEM_LIMIT), name="s5_sample",
    )(u, h0re, h0im, apow, bbd, cre, cim, d, wglu)


def _head_sumsq(x, ones_ref):
    return _dot_exact_rhs(x * x, ones_ref[...])


def _delta_act(conv, ba, gp_ref, ones_ref):
    a = _silu(conv)
    q = a[:, :W]
    k = a[:, W:2 * W]
    v = a[:, 2 * W:]
    q = q * lax.rsqrt(_head_sumsq(q, ones_ref) + RMS_EPS) * (DH ** -0.5)
    k = k * lax.rsqrt(_head_sumsq(k, ones_ref) + RMS_EPS)
    lane = lax.broadcasted_iota(jnp.int32, ba.shape, 1)
    beta = jax.nn.sigmoid(ba)
    g = -jnp.exp(gp_ref[0:1, :]) * _softplus(ba + gp_ref[1:2, :])
    gb = jnp.where(lane < NH, beta, jnp.where(lane < 2 * NH, g, 0.0))
    return q, k, v, gb


CONV_TM = 256


def _convprep_prompt_kernel(c3_ref, qkv_ref, ba_ref, wb_ref, wd_ref, gp_ref, ones_ref,
                            yb_ref, q_ref, k_ref, v_ref, gb_ref, cbs_ref, cds_ref, cbuf, dbuf):
    tm = CONV_TM
    i = pl.program_id(1)

    @pl.when(i == 0)
    def _():
        cbuf[0:8, :] = jnp.zeros((8, W), F32)
        dbuf[0:8, :] = jnp.zeros((8, 3 * W), F32)

    c3 = c3_ref[...]
    cx = c3[:, W:2 * W] * c3[:, 2 * W:]
    cbuf[8:8 + tm, :] = cx
    full = cbuf[...]
    conv = wb_ref[2:3, :] * cx
    for j in (1, 2):
        conv = conv + wb_ref[2 - j:3 - j, :] * pltpu.roll(full, j, 0)[8:]
    yb_ref[...] = c3[:, :W] * conv
    cbs_ref[...] = cbuf[tm:tm + 8, :]
    cbuf[0:8, :] = cbuf[tm:tm + 8, :]

    x = qkv_ref[...]
    dbuf[8:8 + tm, :] = x
    full = dbuf[...]
    conv = wd_ref[3:4, :] * x
    for j in (1, 2, 3):
        conv = conv + wd_ref[3 - j:4 - j, :] * pltpu.roll(full, j, 0)[8:]
    cds_ref[...] = dbuf[tm:tm + 8, :]
    dbuf[0:8, :] = dbuf[tm:tm + 8, :]
    q, k, v, gb = _delta_act(conv, ba_ref[...], gp_ref, ones_ref)
    q_ref[...] = q
    k_ref[...] = k
    v_ref[...] = v
    gb_ref[...] = gb


def _convprep_prompt(c3, qkv, ba, wb, wd, gp, ones_bd):
    b, l, _ = c3.shape
    tm = CONV_TM
    st = lambda c: pl.BlockSpec((None, 8, c), lambda b, i: (b, 0, 0))
    return pl.pallas_call(
        _convprep_prompt_kernel, grid=(b, l // tm),
        in_specs=[_tok_spec(tm, 3 * W), _tok_spec(tm, 3 * W), _tok_spec(tm, 128), _full_spec((3, W)),
                  _full_spec((4, 3 * W)), _full_spec((2, 128)), _full_spec((W, W))],
        out_specs=[_tok_spec(tm, W)] * 4 + [_tok_spec(tm, 128), st(W), st(3 * W)],
        out_shape=[jax.ShapeDtypeStruct((b, l, W), F32)] * 4 +
                  [jax.ShapeDtypeStruct((b, l, 128), F32), jax.ShapeDtypeStruct((b, 8, W), F32),
                   jax.ShapeDtypeStruct((b, 8, 3 * W), F32)],
        scratch_shapes=[pltpu.VMEM((tm + 8, W), F32), pltpu.VMEM((tm + 8, 3 * W), F32)],
        compiler_params=_cp("arbitrary", "arbitrary"), name="convprep_prompt",
    )(c3, qkv, ba, wb, wd, gp, ones_bd)


GDN_CPG = 4
PW = 2 * DH
NPAIR = NH // 2


def _each(fn, *lists):
    return [fn(*args) for args in zip(*lists)]


def _pair_diag(x, blk):
    return jnp.where(blk, jnp.concatenate([x, x], axis=0), 0.0)


def _pair_fold(x):
    n = x.shape[0] // 2
    return x[:n] + x[n:]


def _tri_inv_all(ms, ii, jj):
    eye = (ii == jj).astype(F32)
    blk = (ii // 16) == (jj // 16)
    nd = _each(lambda m: jnp.where(blk, -m, 0.0), ms)
    e = _each(lambda m: jnp.where(blk, 0.0, m), ms)
    n2 = _each(lambda a: _mm3(a, a), nd)
    n4 = _each(lambda a: _mm3(a, a), n2)
    n8 = _each(lambda a: _mm3(a, a), n4)
    d = _each(lambda a, b: _mm3(eye + a, eye + b), nd, n2)
    d = _each(lambda a, b: _mm3(a, eye + b), d, n4)
    dinv = _each(lambda a, b: _mm3(a, eye + b), d, n8)
    x = _each(_mm3, dinv, e)
    x2 = _each(lambda a: _mm3(a, a), x)
    t = _each(lambda a, b: _mm3(eye - a, eye + b), x, x2)
    return _each(_mm3, t, dinv)


def _gdn_local_kernel(q_ref, k_ref, v_ref, gb_ref, uv_ref, w_ref, qk_ref, qd_ref, kdt_ref, egl_ref):
    cs = CHUNK
    n = 2 * cs
    i64 = lax.broadcasted_iota(jnp.int32, (cs, cs), 0)
    j64 = lax.broadcasted_iota(jnp.int32, (cs, cs), 1)
    tri_incl = (i64 >= j64).astype(BF16)
    tri_up = (i64 <= j64).astype(BF16)
    ii = lax.broadcasted_iota(jnp.int32, (n, n), 0)
    jj = lax.broadcasted_iota(jnp.int32, (n, n), 1)
    blk = (ii // cs) == (jj // cs)
    incl = jnp.logical_and(blk, ii >= jj)
    strict = jnp.logical_and(blk, ii > jj)
    top = lax.broadcasted_iota(jnp.int32, (n, 1), 0) < cs
    first = lax.broadcasted_iota(jnp.int32, (cs, PW), 1) < DH
    rows = [slice(c * cs, (c + 1) * cs) for c in range(GDN_CPG)]
    gbc = [gb_ref[r, :] for r in rows]
    gc_all = _each(lambda g: sum(_dot(tri_incl, p) for p in _split(g)), gbc)
    gct_all = _each(lambda g: _dot_exact_rhs(g.T, tri_up), gbc)
    chains = [(c, p) for c in range(GDN_CPG) for p in range(NPAIR)]
    psl = lambda p: slice(p * PW, (p + 1) * PW)
    col2 = lambda m, i: jnp.concatenate([m[:, i:i + 1], m[:, i + 1:i + 2]], axis=0)
    gcol = [col2(gc_all[c], NH + 2 * p) for c, p in chains]
    beta = [col2(gbc[c], 2 * p) for c, p in chains]
    grow = [jnp.concatenate([gct_all[c][NH + 2 * p:NH + 2 * p + 1, :], gct_all[c][NH + 2 * p + 1:NH + 2 * p + 2, :]],
                            axis=1) for c, p in chains]
    qs = [q_ref[rows[c], psl(p)] for c, p in chains]
    qp = [_pair_diag(x, blk) for x in qs]
    kp = [_pair_diag(k_ref[rows[c], psl(p)], blk) for c, p in chains]
    vp = [_pair_diag(v_ref[rows[c], psl(p)], blk) for c, p in chains]
    e = _each(lambda a, b: jnp.exp(jnp.where(incl, a - b, 0.0)), gcol, grow)
    gram = _each(lambda q, k: _mm3_nt(jnp.concatenate([q, k], axis=0), k), qp, kp)
    m = _each(lambda b, g, d: b * g[n:] * jnp.where(strict, d, 0.0), beta, gram, e)
    egc = _each(jnp.exp, gcol)
    rhs = _each(lambda b, v, g, k: jnp.concatenate([b * v, (b * g) * k], axis=1), beta, vp, egc, kp)
    sol = _each(_mm3, _tri_inv_all(m, ii, jj), rhs)
    qk = _each(lambda g, d: _pair_fold(g[:n] * jnp.where(incl, d, 0.0)), gram, e)
    glast = [jnp.where(top, g[cs - 1:cs, :], g[n - 1:n, :]) for g in gcol]
    kdt = _each(lambda k, gl, g: _pair_fold((k * jnp.exp(gl - g)).T), kp, glast, gcol)
    qd = _each(lambda q, g: q * jnp.where(first, g[:cs], g[cs:]), qs, egc)
    egl = [jnp.where(first[:8], jnp.exp(g[cs - 1:cs, :]), jnp.exp(g[n - 1:n, :])) for g in gcol]
    for (c, p), s, a, b, d, g in zip(chains, sol, qk, qd, kdt, egl):
        uv_ref[rows[c], psl(p)] = _pair_fold(s[:, :PW])
        w_ref[rows[c], psl(p)] = _pair_fold(s[:, PW:])
        qk_ref[rows[c], psl(p)] = a
        qd_ref[rows[c], psl(p)] = b
        kdt_ref[rows[c], psl(p)] = d
        egl_ref[c, :, psl(p)] = g


def _gdn_local(q, k, v, gb):
    b, l, _ = q.shape
    r = GDN_CPG * CHUNK
    nc = l // CHUNK
    return pl.pallas_call(
        _gdn_local_kernel, grid=(b, l // r),
        in_specs=[_tok_spec(r, W)] * 3 + [_tok_spec(r, 128)],
        out_specs=[_tok_spec(r, W)] * 5 + [pl.BlockSpec((None, GDN_CPG, 8, W), lambda b, i: (b, i, 0, 0))],
        out_shape=[jax.ShapeDtypeStruct((b, l, W), F32)] * 5 + [jax.ShapeDtypeStruct((b, nc, 8, W), F32)],
        compiler_params=_cp("arbitrary", "arbitrary"), name="gdn_local",
    )(q, k, v, gb)


def _gdn_scan_kernel(uv_ref, w_ref, qk_ref, qd_ref, kdt_ref, egl_ref, z_ref, nw_ref, ones_ref, y_ref, s_ref, s_acc):
    i = pl.program_id(0)
    nb = uv_ref.shape[0]

    @pl.when(i == 0)
    def _():
        s_acc[...] = jnp.zeros(s_acc.shape, F32)

    blk = (lax.broadcasted_iota(jnp.int32, (PW, PW), 0) // DH) == (lax.broadcasted_iota(jnp.int32, (PW, PW), 1) // DH)
    chains = [(b, p) for b in range(nb) for p in range(NPAIR)]
    psl = lambda p: slice(p * PW, (p + 1) * PW)
    load = lambda ref: [ref[b, :, psl(p)] for b, p in chains]
    s = [s_acc[b, p] for b, p in chains]
    u = _each(lambda a, w, st: a - _mm3(w, st), load(uv_ref), load(w_ref), s)
    ubd = [_pair_diag(x, blk) for x in u]
    dot16 = lambda a, b: _dot(a.astype(BF16), b.astype(BF16))
    o = _each(dot16, load(qd_ref), s)
    o = _each(lambda a, qk, ut: a + dot16(qk, ut), o, load(qk_ref), ubd)
    ks = _each(lambda k, ut: _mm3(_pair_diag(k, blk), ut), load(kdt_ref), ubd)
    for (b, p), st, k in zip(chains, s, ks):
        s_acc[b, p] = egl_ref[b, 0, 0:1, psl(p)] * st + k
    zs = load(z_ref)
    for (b, p), a, z in zip(chains, o, zs):
        ms = _dot_exact_rhs(a * a, ones_ref[...]) * (1.0 / DH)
        y_ref[b, :, psl(p)] = a * lax.rsqrt(ms + RMS_EPS) * nw_ref[...] * _silu(z)

    @pl.when(i == pl.num_programs(0) - 1)
    def _():
        for b, p in chains:
            st = s_acc[b, p]
            for a in range(2):
                s_ref[b, 2 * p + a] = st[a * DH:(a + 1) * DH, a * DH:(a + 1) * DH]


def _gdn_scan(uv, w, qk, qd, kdt, egl, z, nw2, ones_pair):
    b, l, _ = uv.shape
    cs = CHUNK
    blk = pl.BlockSpec((b, cs, W), lambda i: (0, i, 0))
    return pl.pallas_call(
        _gdn_scan_kernel, grid=(l // cs,),
        in_specs=[blk] * 5 + [pl.BlockSpec((b, 1, 8, W), lambda i: (0, i, 0, 0)), blk,
                              pl.BlockSpec((1, PW), lambda i: (0, 0)), pl.BlockSpec((PW, PW), lambda i: (0, 0))],
        out_specs=[blk, pl.BlockSpec((b, NH, DH, DH), lambda i: (0, 0, 0, 0))],
        out_shape=[jax.ShapeDtypeStruct((b, l, W), F32), jax.ShapeDtypeStruct((b, NH, DH, DH), F32)],
        scratch_shapes=[pltpu.VMEM((b, NPAIR, PW, PW), F32)],
        compiler_params=_cp("arbitrary"), name="gdn_scan",
    )(uv, w, qk, qd, kdt, egl, z, nw2, ones_pair)


SB_TQ = 256
SB_TK = 128


def _sb_block(z, r, u_ref, mask):
    sp = _softplus(z)
    ln = -sp
    if mask is not None:
        ln = jnp.where(mask, ln, 0.0)
    e = _dot_exact_rhs(ln, u_ref[...])
    w = jnp.exp((z - sp) + e + r)
    if mask is not None:
        w = jnp.where(mask, w, 0.0)
    return w, r + jnp.sum(ln, axis=-1, keepdims=True)


def _sb_prompt_kernel(q_ref, k_ref, vt_ref, a_ref, o_ref, qbd_ref, nz0, nz1, d0, d1, w0, w1, e0, e1, acc_ref):
    i = pl.program_id(1)
    tq, tk = SB_TQ, SB_TK
    nzb, db, wb, eb = (nz0, nz1), (d0, d1), (w0, w1), (e0, e1)
    tiles = [slice(h * tq, (h + 1) * tq) for h in range(NH)]
    qt = (q_ref[...] * (-(DH ** -0.5) * LOG2E)).T
    rowh = lax.broadcasted_iota(jnp.int32, (W, tq), 0) // DH
    for h, t in enumerate(tiles):
        qbd_ref[:, t] = jnp.where(rowh == h, qt, 0.0).astype(BF16)
    kk = lax.broadcasted_iota(jnp.int32, (tk, tq), 0)
    qq = lax.broadcasted_iota(jnp.int32, (tk, tq), 1)

    def stage_z(j, dst):
        kb = k_ref[pl.ds(pl.multiple_of(jnp.maximum(j, 0) * tk, tk), tk), :]
        for t in tiles:
            dst[:, t] = _dot(kb, qbd_ref[:, t])

    def stage_l(src, dst, erow, mask=None):
        for t in tiles:
            nz = src[:, t]
            ln = jnp.minimum(nz, 0.0) - jnp.log2(1.0 + jnp.exp2(-jnp.abs(nz)))
            if mask is not None:
                ln = jnp.where(mask, ln, 0.0)
            et = _dot(a_ref[...], ln.astype(BF16))
            dst[:, t] = et - nz
            erow[:, t] = et[0:1, :]

    def stage_x(src, erow, dst, r, mask=None):
        for t in tiles:
            w = jnp.exp2(src[:, t] + r[:, t])
            if mask is not None:
                w = jnp.where(mask, w, 0.0)
            dst[:, t] = w.astype(BF16)
        return r + erow[...]

    def stage_p(j, src):
        vt = vt_ref[:, pl.ds(pl.multiple_of(j * tk, tk), tk)]
        for h, t in enumerate(tiles):
            acc_ref[h * DH:(h + 1) * DH, :] += _dot(vt[h * DH:(h + 1) * DH, :], src[:, t])

    assert tq == 2 * tk
    acc_ref[...] = jnp.zeros(acc_ref.shape, F32)
    r = jnp.zeros((1, NH * tq), F32)
    jb = (2 * i + 1, 2 * i)
    causal = [(j * tk + kk) < (i * tq + qq) for j in jb]
    for s in range(2):
        stage_z(jb[s], nzb[s])
    for s in range(2):
        stage_l(nzb[s], db[s], eb[s], causal[s])
    for s in range(2):
        r = stage_x(db[s], eb[s], wb[s], r, causal[s])
    for s in range(2):
        stage_p(jb[s], wb[s])
    j0 = 2 * i - 1
    stage_z(j0, nzb[1])
    stage_z(j0 - 1, nzb[0])
    stage_l(nzb[1], db[1], eb[1])
    stage_l(nzb[0], db[0], eb[0])
    r = stage_x(db[1], eb[1], wb[1], r)
    stage_z(j0 - 2, nzb[1])

    def tick(j, p, r):
        stage_p(j, wb[p])
        stage_z(j - 3, nzb[1 - p])
        stage_l(nzb[p], db[p], eb[p])
        return stage_x(db[1 - p], eb[1 - p], wb[1 - p], r)

    def body(t, r):
        j = j0 - 2 * t
        return tick(j - 1, 0, tick(j, 1, r))

    lax.fori_loop(0, i, body, r)
    o_ref[...] = acc_ref[...].T


def _sb_prompt(q, k16, vt16, a_incl):
    b, l, _ = q.shape
    wide = (SB_TK, NH * SB_TQ)
    return pl.pallas_call(
        _sb_prompt_kernel, grid=(b, l // SB_TQ),
        in_specs=[_tok_spec(SB_TQ, W),
                  pl.BlockSpec((None, l, W), lambda b, i: (b, 0, 0)),
                  pl.BlockSpec((None, W, l), lambda b, i: (b, 0, 0)),
                  _full_spec((SB_TK, SB_TK))],
        out_specs=_tok_spec(SB_TQ, W),
        out_shape=jax.ShapeDtypeStruct((b, l, W), F32),
        scratch_shapes=[pltpu.VMEM((W, NH * SB_TQ), BF16)] + [pltpu.VMEM(wide, F32)] * 4 +
                       [pltpu.VMEM(wide, BF16)] * 2 + [pltpu.VMEM((1, NH * SB_TQ), F32)] * 2 +
                       [pltpu.VMEM((W, SB_TQ), F32)],
        compiler_params=_cp("arbitrary", "arbitrary"), name="sb_prompt",
    )(q, k16, vt16, a_incl)


def _strict_upper(n):
    i = jnp.arange(n)
    return (i[:, None] > i[None, :]).astype(BF16)


def _convprep_sample_kernel(c3_ref, qkv_ref, ba_ref, cb0_ref, cd0_ref, wb_ref, wd_ref, gp_ref, ones_ref,
                            yb_ref, qt_ref, kt_ref, vt_ref, gbt_ref, cbs_ref, cds_ref):
    nt = c3_ref.shape[0]
    xb = [cb0_ref[0], cb0_ref[1]]
    xd = [cd0_ref[0], cd0_ref[1], cd0_ref[2]]
    for t in range(nt):
        c3 = c3_ref[t]
        xb.append(c3[:, W:2 * W] * c3[:, 2 * W:])
        xd.append(qkv_ref[t])
    for t in range(nt):
        conv = xb[t] * wb_ref[0:1, :] + xb[t + 1] * wb_ref[1:2, :] + xb[t + 2] * wb_ref[2:3, :]
        yb_ref[t] = c3_ref[t][:, :W] * conv
        conv = (xd[t] * wd_ref[0:1, :] + xd[t + 1] * wd_ref[1:2, :] + xd[t + 2] * wd_ref[2:3, :]
                + xd[t + 3] * wd_ref[3:4, :])
        q, k, v, gb = _delta_act(conv, ba_ref[t], gp_ref, ones_ref)
        qt_ref[t] = q.T
        kt_ref[t] = k.T
        vt_ref[t] = v.T
        gbt_ref[t] = gb.T[0:8, :]
    cbs_ref[0] = xb[nt]
    cbs_ref[1] = xb[nt + 1]
    for j in range(3):
        cds_ref[j] = xd[nt + j]


def _convprep_sample(c3, qkv, ba, cb0, cd0, wb, wd, gp, ones_bd):
    t, s, _ = c3.shape
    sds = jax.ShapeDtypeStruct
    return pl.pallas_call(
        _convprep_sample_kernel,
        out_shape=[sds((t, s, W), F32), sds((t, W, s), F32), sds((t, W, s), F32), sds((t, W, s), F32),
                   sds((t, 8, s), F32), sds((2, s, W), F32), sds((3, s, 3 * W), F32)],
        compiler_params=pltpu.CompilerParams(vmem_limit_bytes=VMEM_LIMIT), name="convprep_sample",
    )(c3, qkv, ba, cb0, cd0, wb, wd, gp, ones_bd)


def _gdn_sample_kernel(qt_ref, kt_ref, vt_ref, gbt_ref, s0_ref, ot_ref, s1_ref):
    h = pl.program_id(0)
    nt = qt_ref.shape[0]
    ns = qt_ref.shape[2]
    for t in range(nt):
        src = s0_ref if t == 0 else s1_ref
        b = gbt_ref[t, pl.ds(h, 1), :]
        a = jnp.exp(gbt_ref[t, pl.ds(NH + h, 1), :])

        def ks_body(i, acc):
            return acc + kt_ref[t, pl.ds(i, 1), :] * src[i]

        ks = lax.fori_loop(0, DH, ks_body, jnp.zeros((DH, ns), F32), unroll=8)
        u = b * (vt_ref[t] - a * ks)

        def up_body(i, acc):
            sn = a * src[i] + kt_ref[t, pl.ds(i, 1), :] * u
            s1_ref[i] = sn
            return acc + qt_ref[t, pl.ds(i, 1), :] * sn

        ot_ref[t] = lax.fori_loop(0, DH, up_body, jnp.zeros((DH, ns), F32), unroll=8)


def _gdn_sample(qt, kt, vt, gbt, s0):
    t, _, s = qt.shape
    hb = pl.BlockSpec((t, DH, s), lambda h: (0, h, 0))
    sb = pl.BlockSpec((None, DH, DH, s), lambda h: (h, 0, 0, 0))
    return pl.pallas_call(
        _gdn_sample_kernel, grid=(NH,),
        in_specs=[hb, hb, hb, pl.BlockSpec((t, 8, s), lambda h: (0, 0, 0)), sb],
        out_specs=[hb, sb],
        out_shape=[jax.ShapeDtypeStruct((t, W, s), F32), jax.ShapeDtypeStruct((NH, DH, DH, s), F32)],
        compiler_params=_cp("arbitrary"), name="gdn_sample",
    )(qt, kt, vt, gbt, s0)


def _gdn_post_sample_kernel(ot_ref, z_ref, nw_ref, ones_ref, y_ref):
    for t in range(ot_ref.shape[0]):
        o = ot_ref[t].T
        ms = _head_sumsq(o, ones_ref) * (1.0 / DH)
        y_ref[t] = o * lax.rsqrt(ms + RMS_EPS) * nw_ref[...] * _silu(z_ref[t])


def _gdn_post_sample(ot, z, nw4, ones_bd):
    t, _, s = ot.shape
    return pl.pallas_call(
        _gdn_post_sample_kernel, out_shape=jax.ShapeDtypeStruct((t, s, W), F32), name="gdn_post_sample",
    )(ot, z, nw4, ones_bd)


def _sb_sample_kernel(pt_ref, q_ref, kn_ref, vn_ref, *rest):
    n_pages = (len(rest) - 4) // 2
    kp = rest[:n_pages]
    vp = rest[n_pages:2 * n_pages]
    u_ref, o_ref, kpad, vpad = rest[2 * n_pages:]
    nt = q_ref.shape[0]
    nq = NH * nt
    row = lax.broadcasted_iota(jnp.int32, (nq, W), 0)
    lane = lax.broadcasted_iota(jnp.int32, (nq, W), 1)
    q = q_ref[...] * (DH ** -0.5)
    qbd = jnp.where(row // nt == lane // DH, jnp.concatenate([q] * NH, axis=0), 0.0).astype(BF16)

    kpad[...] = jnp.zeros(kpad.shape, F32)
    vpad[...] = jnp.zeros(vpad.shape, F32)
    kpad[0:nt, :] = kn_ref[...]
    vpad[0:nt, :] = vn_ref[...]
    qi = lax.broadcasted_iota(jnp.int32, (nq, PAGE), 0) % nt
    ki = lax.broadcasted_iota(jnp.int32, (nq, PAGE), 1)
    z = _dot_nt(qbd, kpad[...].astype(BF16))
    w, r = _sb_block(z, jnp.zeros((nq, 1), F32), u_ref, ki < qi)
    acc = _dot(w.astype(BF16), vpad[...].astype(BF16))

    pages = list(range(n_pages - 1, -1, -1))
    z = [_dot(qbd, kp[p][...].astype(BF16)) for p in pages]
    sp = _each(_softplus, z)
    e = _each(lambda s: _dot_exact_rhs(-s, u_ref[...]), sp)
    for p, zp, spp, ep in zip(pages, z, sp, e):
        w = jnp.exp((zp - spp) + ep + r)
        acc = acc + _dot_nt(w.astype(BF16), vp[p][...].astype(BF16))
        r = r - jnp.sum(spp, axis=-1, keepdims=True)

    sel = jnp.where(row // nt == lane // DH, acc, 0.0)
    out = sel[0:nt]
    for h in range(1, NH):
        out = out + sel[h * nt:(h + 1) * nt]
    o_ref[...] = out


def _sb_sample(page_table, q, kn, vn, cache_kt, cache_vt, layer, u_strict):
    s, t, _ = q.shape
    n_pages = page_table.shape[1]
    seq = pl.BlockSpec((None, t, W), lambda s_, pt: (s_, 0, 0))

    def page_spec(p):
        return pl.BlockSpec((None, None, W, PAGE), lambda s_, pt: (layer, pt[s_, p], 0, 0))

    grid_spec = pltpu.PrefetchScalarGridSpec(
        num_scalar_prefetch=1, grid=(s,),
        in_specs=[seq, seq, seq] + [page_spec(p) for p in range(n_pages)] * 2 +
                 [pl.BlockSpec((PAGE, PAGE), lambda s_, pt: (0, 0))],
        out_specs=seq,
        scratch_shapes=[pltpu.VMEM((PAGE, W), F32), pltpu.VMEM((PAGE, W), F32)])
    return pl.pallas_call(
        _sb_sample_kernel, grid_spec=grid_spec,
        out_shape=jax.ShapeDtypeStruct((s, t, W), F32),
        compiler_params=_cp("arbitrary"), name="sb_sample",
    )(page_table, q, kn, vn, *([cache_kt] * n_pages), *([cache_vt] * n_pages), u_strict)


def _layer_weights(l, w_in, ssm_a_re, ssm_a_im, ssm_log_dt, ssm_b_re, ssm_b_im, ssm_c_re, ssm_c_im, ssm_d, w_glu,
                   conv_b_w, delta_conv_w, delta_a_log, delta_dt_bias, delta_norm_w, w_branch, w_gate, w_o,
                   ln1_g, ln1_b, w_ffn_up, w_ffn_down, ln2_g, ln2_b):
    w = w_in[l]
    w_in_p = jnp.concatenate([w[:, :2048], jnp.pad(w[:, 2048:2056], ((0, 0), (0, 120))), w[:, 2056:]],
                             axis=1).astype(BF16)
    apow, bbd = _s5_prep(ssm_a_re[l], ssm_a_im[l], ssm_log_dt[l], ssm_b_re[l], ssm_b_im[l])
    gp = jnp.zeros((2, 128), F32)
    gp = gp.at[0, NH:2 * NH].set(delta_a_log[l]).at[1, NH:2 * NH].set(delta_dt_bias[l])
    hid = jnp.arange(W) // DH
    return dict(
        w_in=w_in_p, apow=apow, bbd=bbd.astype(BF16), cre=_s5_cmat(ssm_c_re[l]).astype(BF16),
        cim=_s5_cmat(ssm_c_im[l]).astype(BF16), d=ssm_d[l].reshape(1, W), wglu=w_glu[l].astype(BF16),
        wb=conv_b_w[l], wd=delta_conv_w[l], gp=gp, nw=delta_norm_w[l].reshape(1, DH),
        nw4=jnp.tile(delta_norm_w[l], NH).reshape(1, W),
        ones_bd=(hid[:, None] == hid[None, :]).astype(BF16),
        wbr=w_branch[l].astype(BF16), wg=w_gate[l].astype(BF16), wo=w_o[l].astype(BF16),
        ln1g=ln1_g[l].reshape(1, D), ln1b=ln1_b[l].reshape(1, D),
        wu=w_ffn_up[l].astype(BF16), wdn=w_ffn_down[l].astype(BF16),
        ln2g=ln2_g[l].reshape(1, D), ln2b=ln2_b[l].reshape(1, D))


def _prompt_layer(x, mod, p, u_strict):
    b, l, _ = x.shape
    u, c3, qkv, z, ba, q, kt, vt, k16, vt16 = _proj(x, mod, p["w_in"], 512, True)
    ya, hre, him = _s5_prompt(u, p["apow"], p["bbd"], p["cre"], p["cim"], p["d"], p["wglu"])
    yb, qd, kd, vd, gb, cbs, cds = _convprep_prompt(c3, qkv, ba, p["wb"], p["wd"], p["gp"], p["ones_bd"])
    uv, w, qk, qdec, kdt, egl = _gdn_local(qd, kd, vd, gb)
    yc, s_new = _gdn_scan(uv, w, qk, qdec, kdt, egl, z, p["nw4"][:, :PW], p["ones_bd"][:PW, :PW])
    idx = jnp.arange(SB_TK)
    a_incl = (idx[None, :] >= idx[:, None]).astype(BF16)
    yd = _sb_prompt(q, k16, vt16, a_incl)
    x1 = _merge(x, mod, ya, yb, yc, yd, p["wg"], p["wbr"], p["wo"], p["ln1g"], p["ln1b"], 256)
    x2 = _ffn(x1, mod, p["wu"], p["wdn"], p["ln2g"], p["ln2b"], 512)
    unt = lambda t: t.reshape(b, NH, DH, l).transpose(0, 3, 1, 2)
    new = (unt(kt), unt(vt), hre.reshape(b, G, P), him.reshape(b, G, P),
           cbs[:, 6:8], s_new, cds[:, 5:8])
    return x2, new


def _sample_layer(x, mod, p, st, page_table, cache_k, cache_v, layer, u_strict):
    h0re, h0im, cb0, s0, cd0 = st
    s = h0re.shape[0]
    n = x.shape[1]
    t = n // s
    u, c3, qkv, z, ba, q, k, v = _proj(x, mod, p["w_in"], n, False)
    tm = lambda a: a.reshape(t, s, a.shape[-1])
    sm = lambda a: tm(a).transpose(1, 0, 2)
    ya, hre, him = _s5_sample(tm(u), h0re.reshape(s, NS), h0im.reshape(s, NS), p["apow"], p["bbd"], p["cre"],
                              p["cim"], p["d"], p["wglu"])
    yb, qt, kt, vt, gbt, cbs, cds = _convprep_sample(tm(c3), tm(qkv), tm(ba), cb0.transpose(1, 0, 2),
                                                      cd0.transpose(1, 0, 2), p["wb"], p["wd"], p["gp"],
                                                      p["ones_bd"])
    ot, s1 = _gdn_sample(qt, kt, vt, gbt, s0.transpose(1, 2, 3, 0))
    yc = _gdn_post_sample(ot, tm(z), p["nw4"], p["ones_bd"])
    ksm, vsm = sm(k), sm(v)
    yd = _sb_sample(page_table, sm(q), ksm, vsm, cache_k, cache_v, layer, u_strict).transpose(1, 0, 2)
    flat = lambda a: a.reshape(1, n, W)
    x1 = _merge(x, mod, flat(ya), flat(yb), flat(yc), flat(yd), p["wg"], p["wbr"], p["wo"], p["ln1g"], p["ln1b"],
                min(256, n))
    x2 = _ffn(x1, mod, p["wu"], p["wdn"], p["ln2g"], p["ln2b"], n)
    new = (ksm.reshape(s, t, NH, DH), vsm.reshape(s, t, NH, DH), hre.reshape(s, G, P), him.reshape(s, G, P),
           cbs.transpose(1, 0, 2), s1.transpose(3, 0, 1, 2), cds.transpose(1, 0, 2))
    return x2, new


def kernel(x_prompt, x_sample, cache_k, cache_v, state_ssm_re, state_ssm_im, state_conv_b, state_delta,
           state_conv_delta, page_table, c_prompt, c_sample, w_ada, b_ada, w_in, ssm_a_re, ssm_a_im, ssm_log_dt,
           ssm_b_re, ssm_b_im, ssm_c_re, ssm_c_im, ssm_d, w_glu, conv_b_w, delta_conv_w, delta_a_log,
           delta_dt_bias, delta_norm_w, w_branch, w_gate, w_o, ln1_g, ln1_b, w_ffn_up, w_ffn_down, ln2_g, ln2_b):
    depth = w_ada.shape[0]
    bp, lp, _ = x_prompt.shape
    bs, ts, _ = x_sample.shape
    n_pool = cache_k.shape[1]

    c_all = jnp.concatenate([c_prompt, c_sample], axis=0)
    pad = (-c_all.shape[0]) % 8
    mod = _mod(jnp.pad(c_all, ((0, pad), (0, 0))), w_ada, b_ada)
    mod_p = mod[:, :bp].reshape(depth, bp, 6, 1, D)
    mod_s = mod[:, bp:bp + bs].reshape(depth, bs, 6, D).transpose(0, 2, 1, 3)
    mod_s = jnp.broadcast_to(mod_s[:, :, None], (depth, 6, ts, bs, D)).reshape(depth, 1, 6, ts * bs, D)

    ck = cache_k.transpose(0, 1, 3, 4, 2).reshape(depth, n_pool, W, PAGE)
    cv = cache_v.transpose(0, 1, 3, 4, 2).reshape(depth, n_pool, W, PAGE)
    u_strict = _strict_upper(PAGE)

    x_p = x_prompt
    x_s = x_sample.transpose(1, 0, 2).reshape(1, ts * bs, D)
    new_p, new_s = [], []
    for l in range(depth):
        p = _layer_weights(l, w_in, ssm_a_re, ssm_a_im, ssm_log_dt, ssm_b_re, ssm_b_im, ssm_c_re, ssm_c_im, ssm_d,
                           w_glu, conv_b_w, delta_conv_w, delta_a_log, delta_dt_bias, delta_norm_w, w_branch,
                           w_gate, w_o, ln1_g, ln1_b, w_ffn_up, w_ffn_down, ln2_g, ln2_b)
        x_p, np_ = _prompt_layer(x_p, mod_p[l], p, u_strict)
        new_p.append(np_)
        st = (state_ssm_re[l], state_ssm_im[l], state_conv_b[l], state_delta[l], state_conv_delta[l])
        x_s, ns_ = _sample_layer(x_s, mod_s[l], p, st, page_table, ck, cv, l, u_strict)
        new_s.append(ns_)
    k_p, v_p, re_p, im_p, cb_p, d_p, cd_p = [jnp.stack(t, axis=0) for t in zip(*new_p)]
    k_s, v_s, re_s, im_s, cb_s, d_s, cd_s = [jnp.stack(t, axis=0) for t in zip(*new_s)]
    y_s = x_s.reshape(ts, bs, D).transpose(1, 0, 2)
    return (x_p, y_s, k_p, v_p, k_s, v_s, re_p, im_p, re_s, im_s, cb_p, cb_s, d_p, d_s, cd_p, cd_s)
```

```python
import functools
import math

import jax
import jax.numpy as jnp
from jax import lax
from jax.experimental import pallas as pl
from jax.experimental.pallas import tpu as pltpu

F32 = jnp.float32
BF16 = jnp.bfloat16

D = 1024
W = 256
NH = 4
DH = 64
G = 16
GW = 16
P = 64
NS = G * P
DFF = 2816
CHUNK = 64
PAGE = 128
ALPHA = (2 * 2) ** 0.25
LN_EPS = 1e-5
RMS_EPS = 1e-6
LOG2E = 1.4426950408889634
VMEM_LIMIT = 56 * 1024 * 1024

SEGS = (("ssm", 256, 0, 256), ("conv", 768, 256, 768), ("qkv", 768, 1024, 768), ("z", 256, 1792, 256),
        ("ba", 128, 2048, 8), ("q", 256, 2056, 256), ("k", 256, 2312, 256), ("v", 256, 2568, 256))
NP_IN = sum(s[1] for s in SEGS)


def _cp(*sem):
    return pltpu.CompilerParams(dimension_semantics=sem, vmem_limit_bytes=VMEM_LIMIT)


def _dot(a, b):
    return jnp.dot(a, b, preferred_element_type=F32)


def _dot_nt(a, b):
    return lax.dot_general(a, b, (((1,), (1,)), ((), ())), preferred_element_type=F32)


def _split(x):
    hi = x.astype(BF16)
    lo = (x - hi.astype(F32)).astype(BF16)
    return hi, lo


def _dot_exact_rhs(x, m_bf16):
    hi, lo = _split(x)
    return _dot(hi, m_bf16) + _dot(lo, m_bf16)


def _mm3(a, b):
    ah, al = _split(a)
    bh, bl = _split(b)
    return _dot(ah, bh) + (_dot(ah, bl) + _dot(al, bh))


def _mm3_nt(a, b):
    ah, al = _split(a)
    bh, bl = _split(b)
    return _dot_nt(ah, bh) + (_dot_nt(ah, bl) + _dot_nt(al, bh))


def _ln(x):
    mu = jnp.mean(x, axis=-1, keepdims=True)
    xc = x - mu
    var = jnp.mean(xc * xc, axis=-1, keepdims=True)
    return xc * lax.rsqrt(var + LN_EPS)


def _softplus(x):
    return jnp.maximum(x, 0.0) + jnp.log1p(jnp.exp(-jnp.abs(x)))


def _silu(x):
    return x * jax.nn.sigmoid(x)


def _mod_kernel(c_ref, w_ref, b_ref, o_ref):
    s = _silu(c_ref[...]).astype(BF16)
    o_ref[0] = _dot(s, w_ref[0].astype(BF16)) + b_ref[0]


def _mod(c_all, w_ada, b_ada):
    mp = c_all.shape[0]
    depth = w_ada.shape[0]
    tn = 1024
    return pl.pallas_call(
        _mod_kernel, grid=(depth, 6 * D // tn),
        in_specs=[pl.BlockSpec((mp, D), lambda l, j: (0, 0)),
                  pl.BlockSpec((1, D, tn), lambda l, j: (l, 0, j)),
                  pl.BlockSpec((1, 1, tn), lambda l, j: (l, 0, j))],
        out_specs=pl.BlockSpec((1, mp, tn), lambda l, j: (l, 0, j)),
        out_shape=jax.ShapeDtypeStruct((depth, mp, 6 * D), F32),
        compiler_params=_cp("arbitrary", "arbitrary"), name="ada_mod",
    )(c_all, w_ada, b_ada.reshape(depth, 1, 6 * D))


def _mod_spec(mod, tm):
    if mod.shape[2] == 1:
        return pl.BlockSpec((None, 6, 1, D), lambda b, i: (b, 0, 0, 0))
    return pl.BlockSpec((None, 6, tm, D), lambda b, i: (b, 0, i, 0))


def _tok_spec(tm, c):
    return pl.BlockSpec((None, tm, c), lambda b, i: (b, i, 0))


def _full_spec(shape):
    n = len(shape)
    return pl.BlockSpec(shape, lambda b, i: (0,) * n)


def _proj_kernel(x_ref, mod_ref, w_ref, *outs):
    h = (_ln(x_ref[...]) * (1.0 + mod_ref[1]) + mod_ref[0]).astype(BF16)
    off = 0
    transposed_kv = len(outs) > len(SEGS)
    for o_ref, seg in zip(outs, SEGS):
        y = _dot(h, w_ref[:, off:off + seg[1]])
        off += seg[1]
        if transposed_kv and seg[0] in ("k", "v"):
            yt = y.T
            o_ref[...] = yt
            if seg[0] == "k":
                outs[len(SEGS)][...] = y.astype(BF16)
            else:
                outs[len(SEGS) + 1][...] = yt.astype(BF16)
        else:
            o_ref[...] = y


def _proj(x, mod, w_in_p, tm, transposed_kv):
    b, l, _ = x.shape
    out_specs = [_tok_spec(tm, s[1]) for s in SEGS]
    out_shape = [jax.ShapeDtypeStruct((b, l, s[1]), F32) for s in SEGS]
    if transposed_kv:
        tspec = pl.BlockSpec((None, W, tm), lambda b, i: (b, 0, i))
        out_specs = out_specs[:-2] + [tspec, tspec, _tok_spec(tm, W), tspec]
        out_shape = out_shape[:-2] + [jax.ShapeDtypeStruct((b, W, l), F32)] * 2 + \
            [jax.ShapeDtypeStruct((b, l, W), BF16), jax.ShapeDtypeStruct((b, W, l), BF16)]
    return pl.pallas_call(
        _proj_kernel, grid=(b, l // tm),
        in_specs=[_tok_spec(tm, D), _mod_spec(mod, tm), _full_spec((D, NP_IN))],
        out_specs=out_specs, out_shape=out_shape,
        compiler_params=_cp("arbitrary", "arbitrary"), name="in_proj",
    )(x, mod, w_in_p)


def _merge_kernel(x_ref, mod_ref, ya_ref, yb_ref, yc_ref, yd_ref, wg_ref, wb_ref, wo_ref, lg_ref, lb_ref, o_ref):
    x = x_ref[...]
    h = (_ln(x) * (1.0 + mod_ref[1]) + mod_ref[0]).astype(BF16)
    acc = None
    for n, y_ref in enumerate((ya_ref, yb_ref, yc_ref, yd_ref)):
        gate = jax.nn.sigmoid(_dot(h, wg_ref[:, n * D:(n + 1) * D]))
        br = _dot(y_ref[...].astype(BF16), wb_ref[n])
        acc = gate * br if acc is None else acc + gate * br
    mixed = _dot(acc.astype(BF16), wo_ref[...])
    r = ALPHA * x + (1.0 + mod_ref[2]) * mixed
    o_ref[...] = _ln(r) * lg_ref[...] + lb_ref[...]


def _merge(x, mod, ya, yb, yc, yd, wg, wb, wo, lg, lb, tm):
    b, l, _ = x.shape
    return pl.pallas_call(
        _merge_kernel, grid=(b, l // tm),
        in_specs=[_tok_spec(tm, D), _mod_spec(mod, tm)] + [_tok_spec(tm, W)] * 4 +
                 [_full_spec((D, 4 * D)), _full_spec((4, W, D)), _full_spec((D, D)),
                  _full_spec((1, D)), _full_spec((1, D))],
        out_specs=_tok_spec(tm, D),
        out_shape=jax.ShapeDtypeStruct((b, l, D), F32),
        compiler_params=_cp("arbitrary", "arbitrary"), name="merge",
    )(x, mod, ya, yb, yc, yd, wg, wb, wo, lg, lb)


FF_CHUNK = 256


def _ffn_kernel(x_ref, mod_ref, wu_ref, wd_ref, lg_ref, lb_ref, o_ref):
    x = x_ref[...]
    h = (_ln(x) * (1.0 + mod_ref[4]) + mod_ref[3]).astype(BF16)
    acc = None
    for c in range(DFF // FF_CHUNK):
        lo = c * FF_CHUNK
        up_a = _dot(h, wu_ref[:, lo:lo + FF_CHUNK])
        up_b = _dot(h, wu_ref[:, DFF + lo:DFF + lo + FF_CHUNK])
        t = (_silu(up_a) * up_b).astype(BF16)
        d = _dot(t, wd_ref[lo:lo + FF_CHUNK, :])
        acc = d if acc is None else acc + d
    r = ALPHA * x + (1.0 + mod_ref[5]) * acc
    o_ref[...] = _ln(r) * lg_ref[...] + lb_ref[...]


def _ffn(x, mod, wu, wd, lg, lb, tm):
    b, l, _ = x.shape
    return pl.pallas_call(
        _ffn_kernel, grid=(b, l // tm),
        in_specs=[_tok_spec(tm, D), _mod_spec(mod, tm), _full_spec((D, 2 * DFF)), _full_spec((DFF, D)),
                  _full_spec((1, D)), _full_spec((1, D))],
        out_specs=_tok_spec(tm, D),
        out_shape=jax.ShapeDtypeStruct((b, l, D), F32),
        compiler_params=_cp("arbitrary", "arbitrary"), name="ffn",
    )(x, mod, wu, wd, lg, lb)


APOW_LEVELS = 8


def _s5_prep_kernel(are_ref, aim_ref, ldt_ref, bre_ref, bim_ref, apow_ref, bbre_ref, bbim_ref):
    a_re = are_ref[...]
    a_im = aim_ref[...]
    dt = jnp.exp(ldt_ref[...])
    mag = jnp.exp(dt * a_re)
    ab_re = mag * jnp.cos(dt * a_im)
    ab_im = mag * jnp.sin(dt * a_im)
    den = a_re * a_re + a_im * a_im
    f_re = ((ab_re - 1.0) * a_re + ab_im * a_im) / den
    f_im = (ab_im * a_re - (ab_re - 1.0) * a_im) / den
    b_re = bre_ref[...]
    b_im = bim_ref[...]
    bbre_ref[...] = f_re * b_re - f_im * b_im
    bbim_ref[...] = f_re * b_im + f_im * b_re
    pr, pi = ab_re, ab_im
    for k in range(APOW_LEVELS):
        apow_ref[2 * k] = pr
        apow_ref[2 * k + 1] = pi
        pr, pi = pr * pr - pi * pi, 2.0 * pr * pi


def _s5_prep(a_re, a_im, log_dt, b_re, b_im):
    rep = lambda t: jnp.repeat(t, GW, axis=0)
    ldt = jnp.broadcast_to(log_dt[:, None], (G, P))
    bt = lambda t: jnp.transpose(t, (0, 2, 1)).reshape(G * GW, P)
    apow_x, bb_re, bb_im = pl.pallas_call(
        _s5_prep_kernel,
        out_shape=[jax.ShapeDtypeStruct((2 * APOW_LEVELS, G * GW, P), F32), jax.ShapeDtypeStruct((G * GW, P), F32),
                   jax.ShapeDtypeStruct((G * GW, P), F32)], name="s5_prep",
    )(rep(a_re), rep(a_im), rep(ldt), bt(b_re), bt(b_im))
    apow = apow_x[:, ::GW, :].reshape(2 * APOW_LEVELS, NS)
    eye = jnp.eye(G, dtype=F32)
    bd = lambda t: jnp.einsum("ghp,gk->ghkp", t.reshape(G, GW, P), eye).reshape(W, NS)
    return apow, jnp.concatenate([bd(bb_re), bd(bb_im)], axis=1)


def _s5_cmat(c):
    eye = jnp.eye(G, dtype=F32)
    return jnp.einsum("ghp,gk->gpkh", c, eye).reshape(NS, W)


def _s5_glu(y_state, u, d_ref, wglu_ref):
    ys = jax.nn.gelu(y_state + d_ref[...] * u)
    return ys * jax.nn.sigmoid(_dot(ys.astype(BF16), wglu_ref[...]))


def _s5_out(hr, hi, u, cre_ref, cim_ref, d_ref, wglu_ref):
    y = _dot(hr.astype(BF16), cre_ref[...]) - _dot(hi.astype(BF16), cim_ref[...])
    return _s5_glu(y, u, d_ref, wglu_ref)


S5_TM = 256
LANES = 128


def _s5_prompt_kernel(u_ref, apow_ref, bbd_ref, cre_ref, cim_ref, d_ref, wglu_ref,
                      ya_ref, hre_ref, him_ref, sbuf, hprev, hbuf):
    tm = S5_TM
    ng = tm // 8
    i = pl.program_id(1)

    @pl.when(i == 0)
    def _():
        hprev[...] = jnp.zeros((8, 2 * NS), F32)

    u = u_ref[...]
    nl = NS // LANES
    bu = _dot(u.astype(BF16), bbd_ref[...])
    for c in range(2 * nl):
        sbuf[c] = bu[:, c * LANES:(c + 1) * LANES]
    row = lax.broadcasted_iota(jnp.int32, (ng, LANES), 0)
    cmul = lambda ar, ai, xr, xi: (ar * xr - ai * xi, ar * xi + ai * xr)
    for c in range(nl):
        cr = slice(c * LANES, (c + 1) * LANES)
        ci = slice(NS + c * LANES, NS + (c + 1) * LANES)
        a1r = apow_ref[0:1, cr]
        a1i = apow_ref[1:2, cr]
        xr = [sbuf[c, pl.ds(s, ng, stride=8), :] for s in range(8)]
        xi = [sbuf[nl + c, pl.ds(s, ng, stride=8), :] for s in range(8)]
        wr, wi = xr[0], xi[0]
        for s in range(1, 8):
            pr, pi = cmul(a1r, a1i, wr, wi)
            wr, wi = pr + xr[s], pi + xi[s]
        gr = jnp.where(row == 0, hprev[0:1, cr], pltpu.roll(wr, 1, 0))
        gi = jnp.where(row == 0, hprev[0:1, ci], pltpu.roll(wi, 1, 0))
        for k, shift in enumerate([1 << b for b in range(ng.bit_length() - 1)]):
            ar = apow_ref[6 + 2 * k:7 + 2 * k, cr]
            ai = apow_ref[7 + 2 * k:8 + 2 * k, cr]
            pr, pi = cmul(ar, ai, jnp.where(row >= shift, pltpu.roll(gr, shift, 0), 0.0),
                          jnp.where(row >= shift, pltpu.roll(gi, shift, 0), 0.0))
            gr, gi = gr + pr, gi + pi
        hr, hi = gr, gi
        for s in range(8):
            pr, pi = cmul(a1r, a1i, hr, hi)
            hr, hi = pr + xr[s], pi + xi[s]
            hbuf[c, pl.ds(s, ng, stride=8), :] = hr
            hbuf[nl + c, pl.ds(s, ng, stride=8), :] = hi
        hprev[0:1, cr] = hr[ng - 1:ng]
        hprev[0:1, ci] = hi[ng - 1:ng]
    h_re = jnp.concatenate([hbuf[c] for c in range(nl)], axis=1)
    h_im = jnp.concatenate([hbuf[nl + c] for c in range(nl)], axis=1)
    ya_ref[...] = _s5_out(h_re, h_im, u, cre_ref, cim_ref, d_ref, wglu_ref)

    @pl.when(i == pl.num_programs(1) - 1)
    def _():
        hre_ref[...] = hprev[0:1, :NS]
        him_ref[...] = hprev[0:1, NS:]


def _s5_prompt(u, apow, bbd, cre, cim, d, wglu):
    b, l, _ = u.shape
    tm = S5_TM
    return pl.pallas_call(
        _s5_prompt_kernel, grid=(b, l // tm),
        in_specs=[_tok_spec(tm, W), _full_spec((2 * APOW_LEVELS, NS)), _full_spec((W, 2 * NS)), _full_spec((NS, W)),
                  _full_spec((NS, W)), _full_spec((1, W)), _full_spec((W, W))],
        out_specs=[_tok_spec(tm, W), pl.BlockSpec((None, 1, NS), lambda b, i: (b, 0, 0)),
                   pl.BlockSpec((None, 1, NS), lambda b, i: (b, 0, 0))],
        out_shape=[jax.ShapeDtypeStruct((b, l, W), F32), jax.ShapeDtypeStruct((b, 1, NS), F32),
                   jax.ShapeDtypeStruct((b, 1, NS), F32)],
        scratch_shapes=[pltpu.VMEM((2 * NS // LANES, tm, LANES), F32), pltpu.VMEM((8, 2 * NS), F32),
                        pltpu.VMEM((2 * NS // LANES, tm, LANES), F32)],
        compiler_params=_cp("arbitrary", "arbitrary"), name="s5_prompt",
    )(u, apow, bbd, cre, cim, d, wglu)


def _s5_sample_kernel(u_ref, h0re_ref, h0im_ref, apow_ref, bbd_ref, cre_ref, cim_ref, d_ref, wglu_ref,
                      ya_ref, hre_ref, him_ref):
    ar = apow_ref[0:1, :]
    ai = apow_ref[1:2, :]
    hr = h0re_ref[...]
    hi = h0im_ref[...]
    for t in range(u_ref.shape[0]):
        u = u_ref[t]
        bu = _dot(u.astype(BF16), bbd_ref[...])
        hr, hi = ar * hr - ai * hi + bu[:, :NS], ar * hi + ai * hr + bu[:, NS:]
        ya_ref[t] = _s5_out(hr, hi, u, cre_ref, cim_ref, d_ref, wglu_ref)
    hre_ref[...] = hr
    him_ref[...] = hi


def _s5_sample(u, h0re, h0im, apow, bbd, cre, cim, d, wglu):
    t, s, _ = u.shape
    return pl.pallas_call(
        _s5_sample_kernel,
        out_shape=[jax.ShapeDtypeStruct((t, s, W), F32), jax.ShapeDtypeStruct((s, NS), F32),
                   jax.ShapeDtypeStruct((s, NS), F32)],
        compiler_params=pltpu.CompilerParams(vmem_limit_bytes=VMEM_LIMIT), name="s5_sample",
    )(u, h0re, h0im, apow, bbd, cre, cim, d, wglu)


def _head_sumsq(x, ones_ref):
    return _dot_exact_rhs(x * x, ones_ref[...])


def _delta_act(conv, ba, gp_ref, ones_ref):
    a = _silu(conv)
    q = a[:, :W]
    k = a[:, W:2 * W]
    v = a[:, 2 * W:]
    q = q * lax.rsqrt(_head_sumsq(q, ones_ref) + RMS_EPS) * (DH ** -0.5)
    k = k * lax.rsqrt(_head_sumsq(k, ones_ref) + RMS_EPS)
    lane = lax.broadcasted_iota(jnp.int32, ba.shape, 1)
    beta = jax.nn.sigmoid(ba)
    g = -jnp.exp(gp_ref[0:1, :]) * _softplus(ba + gp_ref[1:2, :])
    gb = jnp.where(lane < NH, beta, jnp.where(lane < 2 * NH, g, 0.0))
    return q, k, v, gb


CONV_TM = 256


def _convprep_prompt_kernel(c3_ref, qkv_ref, ba_ref, wb_ref, wd_ref, gp_ref, ones_ref,
                            yb_ref, q_ref, k_ref, v_ref, gb_ref, cbs_ref, cds_ref, cbuf, dbuf):
    tm = CONV_TM
    i = pl.program_id(1)

    @pl.when(i == 0)
    def _():
        cbuf[0:8, :] = jnp.zeros((8, W), F32)
        dbuf[0:8, :] = jnp.zeros((8, 3 * W), F32)

    c3 = c3_ref[...]
    cx = c3[:, W:2 * W] * c3[:, 2 * W:]
    cbuf[8:8 + tm, :] = cx
    full = cbuf[...]
    conv = wb_ref[2:3, :] * cx
    for j in (1, 2):
        conv = conv + wb_ref[2 - j:3 - j, :] * pltpu.roll(full, j, 0)[8:]
    yb_ref[...] = c3[:, :W] * conv
    cbs_ref[...] = cbuf[tm:tm + 8, :]
    cbuf[0:8, :] = cbuf[tm:tm + 8, :]

    x = qkv_ref[...]
    dbuf[8:8 + tm, :] = x
    full = dbuf[...]
    conv = wd_ref[3:4, :] * x
    for j in (1, 2, 3):
        conv = conv + wd_ref[3 - j:4 - j, :] * pltpu.roll(full, j, 0)[8:]
    cds_ref[...] = dbuf[tm:tm + 8, :]
    dbuf[0:8, :] = dbuf[tm:tm + 8, :]
    q, k, v, gb = _delta_act(conv, ba_ref[...], gp_ref, ones_ref)
    q_ref[...] = q
    k_ref[...] = k
    v_ref[...] = v
    gb_ref[...] = gb


def _convprep_prompt(c3, qkv, ba, wb, wd, gp, ones_bd):
    b, l, _ = c3.shape
    tm = CONV_TM
    st = lambda c: pl.BlockSpec((None, 8, c), lambda b, i: (b, 0, 0))
    return pl.pallas_call(
        _convprep_prompt_kernel, grid=(b, l // tm),
        in_specs=[_tok_spec(tm, 3 * W), _tok_spec(tm, 3 * W), _tok_spec(tm, 128), _full_spec((3, W)),
                  _full_spec((4, 3 * W)), _full_spec((2, 128)), _full_spec((W, W))],
        out_specs=[_tok_spec(tm, W)] * 4 + [_tok_spec(tm, 128), st(W), st(3 * W)],
        out_shape=[jax.ShapeDtypeStruct((b, l, W), F32)] * 4 +
                  [jax.ShapeDtypeStruct((b, l, 128), F32), jax.ShapeDtypeStruct((b, 8, W), F32),
                   jax.ShapeDtypeStruct((b, 8, 3 * W), F32)],
        scratch_shapes=[pltpu.VMEM((tm + 8, W), F32), pltpu.VMEM((tm + 8, 3 * W), F32)],
        compiler_params=_cp("arbitrary", "arbitrary"), name="convprep_prompt",
    )(c3, qkv, ba, wb, wd, gp, ones_bd)


GDN_CPG = 4
PW = 2 * DH
NPAIR = NH // 2


def _each(fn, *lists):
    return [fn(*args) for args in zip(*lists)]


def _pair_diag(x, blk):
    return jnp.where(blk, jnp.concatenate([x, x], axis=0), 0.0)


def _pair_fold(x):
    n = x.shape[0] // 2
    return x[:n] + x[n:]


def _tri_inv_all(ms, ii, jj):
    eye = (ii == jj).astype(F32)
    blk = (ii // 16) == (jj // 16)
    nd = _each(lambda m: jnp.where(blk, -m, 0.0), ms)
    e = _each(lambda m: jnp.where(blk, 0.0, m), ms)
    n2 = _each(lambda a: _mm3(a, a), nd)
    n4 = _each(lambda a: _mm3(a, a), n2)
    n8 = _each(lambda a: _mm3(a, a), n4)
    d = _each(lambda a, b: _mm3(eye + a, eye + b), nd, n2)
    d = _each(lambda a, b: _mm3(a, eye + b), d, n4)
    dinv = _each(lambda a, b: _mm3(a, eye + b), d, n8)
    x = _each(_mm3, dinv, e)
    x2 = _each(lambda a: _mm3(a, a), x)
    t = _each(lambda a, b: _mm3(eye - a, eye + b), x, x2)
    return _each(_mm3, t, dinv)


def _gdn_local_kernel(q_ref, k_ref, v_ref, gb_ref, uv_ref, w_ref, qk_ref, qd_ref, kdt_ref, egl_ref):
    cs = CHUNK
    n = 2 * cs
    i64 = lax.broadcasted_iota(jnp.int32, (cs, cs), 0)
    j64 = lax.broadcasted_iota(jnp.int32, (cs, cs), 1)
    tri_incl = (i64 >= j64).astype(BF16)
    tri_up = (i64 <= j64).astype(BF16)
    ii = lax.broadcasted_iota(jnp.int32, (n, n), 0)
    jj = lax.broadcasted_iota(jnp.int32, (n, n), 1)
    blk = (ii // cs) == (jj // cs)
    incl = jnp.logical_and(blk, ii >= jj)
    strict = jnp.logical_and(blk, ii > jj)
    top = lax.broadcasted_iota(jnp.int32, (n, 1), 0) < cs
    first = lax.broadcasted_iota(jnp.int32, (cs, PW), 1) < DH
    rows = [slice(c * cs, (c + 1) * cs) for c in range(GDN_CPG)]
    gbc = [gb_ref[r, :] for r in rows]
    gc_all = _each(lambda g: sum(_dot(tri_incl, p) for p in _split(g)), gbc)
    gct_all = _each(lambda g: _dot_exact_rhs(g.T, tri_up), gbc)
    chains = [(c, p) for c in range(GDN_CPG) for p in range(NPAIR)]
    psl = lambda p: slice(p * PW, (p + 1) * PW)
    col2 = lambda m, i: jnp.concatenate([m[:, i:i + 1], m[:, i + 1:i + 2]], axis=0)
    gcol = [col2(gc_all[c], NH + 2 * p) for c, p in chains]
    beta = [col2(gbc[c], 2 * p) for c, p in chains]
    grow = [jnp.concatenate([gct_all[c][NH + 2 * p:NH + 2 * p + 1, :], gct_all[c][NH + 2 * p + 1:NH + 2 * p + 2, :]],
                            axis=1) for c, p in chains]
    qs = [q_ref[rows[c], psl(p)] for c, p in chains]
    qp = [_pair_diag(x, blk) for x in qs]
    kp = [_pair_diag(k_ref[rows[c], psl(p)], blk) for c, p in chains]
    vp = [_pair_diag(v_ref[rows[c], psl(p)], blk) for c, p in chains]
    e = _each(lambda a, b: jnp.exp(jnp.where(incl, a - b, 0.0)), gcol, grow)
    gram = _each(lambda q, k: _mm3_nt(jnp.concatenate([q, k], axis=0), k), qp, kp)
    m = _each(lambda b, g, d: b * g[n:] * jnp.where(strict, d, 0.0), beta, gram, e)
    egc = _each(jnp.exp, gcol)
    rhs = _each(lambda b, v, g, k: jnp.concatenate([b * v, (b * g) * k], axis=1), beta, vp, egc, kp)
    sol = _each(_mm3, _tri_inv_all(m, ii, jj), rhs)
    qk = _each(lambda g, d: _pair_fold(g[:n] * jnp.where(incl, d, 0.0)), gram, e)
    glast = [jnp.where(top, g[cs - 1:cs, :], g[n - 1:n, :]) for g in gcol]
    kdt = _each(lambda k, gl, g: _pair_fold((k * jnp.exp(gl - g)).T), kp, glast, gcol)
    qd = _each(lambda q, g: q * jnp.where(first, g[:cs], g[cs:]), qs, egc)
    egl = [jnp.where(first[:8], jnp.exp(g[cs - 1:cs, :]), jnp.exp(g[n - 1:n, :])) for g in gcol]
    for (c, p), s, a, b, d, g in zip(chains, sol, qk, qd, kdt, egl):
        uv_ref[rows[c], psl(p)] = _pair_fold(s[:, :PW])
        w_ref[rows[c], psl(p)] = _pair_fold(s[:, PW:])
        qk_ref[rows[c], psl(p)] = a
        qd_ref[rows[c], psl(p)] = b
        kdt_ref[rows[c], psl(p)] = d
        egl_ref[c, :, psl(p)] = g


def _gdn_local(q, k, v, gb):
    b, l, _ = q.shape
    r = GDN_CPG * CHUNK
    nc = l // CHUNK
    return pl.pallas_call(
        _gdn_local_kernel, grid=(b, l // r),
        in_specs=[_tok_spec(r, W)] * 3 + [_tok_spec(r, 128)],
        out_specs=[_tok_spec(r, W)] * 5 + [pl.BlockSpec((None, GDN_CPG, 8, W), lambda b, i: (b, i, 0, 0))],
        out_shape=[jax.ShapeDtypeStruct((b, l, W), F32)] * 5 + [jax.ShapeDtypeStruct((b, nc, 8, W), F32)],
        compiler_params=_cp("arbitrary", "arbitrary"), name="gdn_local",
    )(q, k, v, gb)


def _gdn_scan_kernel(uv_ref, w_ref, qk_ref, qd_ref, kdt_ref, egl_ref, z_ref, nw_ref, ones_ref, y_ref, s_ref, s_acc):
    i = pl.program_id(0)
    nb = uv_ref.shape[0]

    @pl.when(i == 0)
    def _():
        s_acc[...] = jnp.zeros(s_acc.shape, F32)

    blk = (lax.broadcasted_iota(jnp.int32, (PW, PW), 0) // DH) == (lax.broadcasted_iota(jnp.int32, (PW, PW), 1) // DH)
    chains = [(b, p) for b in range(nb) for p in range(NPAIR)]
    psl = lambda p: slice(p * PW, (p + 1) * PW)
    load = lambda ref: [ref[b, :, psl(p)] for b, p in chains]
    s = [s_acc[b, p] for b, p in chains]
    u = _each(lambda a, w, st: a - _mm3(w, st), load(uv_ref), load(w_ref), s)
    ubd = [_pair_diag(x, blk) for x in u]
    dot16 = lambda a, b: _dot(a.astype(BF16), b.astype(BF16))
    o = _each(dot16, load(qd_ref), s)
    o = _each(lambda a, qk, ut: a + dot16(qk, ut), o, load(qk_ref), ubd)
    ks = _each(lambda k, ut: _mm3(_pair_diag(k, blk), ut), load(kdt_ref), ubd)
    for (b, p), st, k in zip(chains, s, ks):
        s_acc[b, p] = egl_ref[b, 0, 0:1, psl(p)] * st + k
    zs = load(z_ref)
    for (b, p), a, z in zip(chains, o, zs):
        ms = _dot_exact_rhs(a * a, ones_ref[...]) * (1.0 / DH)
        y_ref[b, :, psl(p)] = a * lax.rsqrt(ms + RMS_EPS) * nw_ref[...] * _silu(z)

    @pl.when(i == pl.num_programs(0) - 1)
    def _():
        for b, p in chains:
            st = s_acc[b, p]
            for a in range(2):
                s_ref[b, 2 * p + a] = st[a * DH:(a + 1) * DH, a * DH:(a + 1) * DH]


def _gdn_scan(uv, w, qk, qd, kdt, egl, z, nw2, ones_pair):
    b, l, _ = uv.shape
    cs = CHUNK
    blk = pl.BlockSpec((b, cs, W), lambda i: (0, i, 0))
    return pl.pallas_call(
        _gdn_scan_kernel, grid=(l // cs,),
        in_specs=[blk] * 5 + [pl.BlockSpec((b, 1, 8, W), lambda i: (0, i, 0, 0)), blk,
                              pl.BlockSpec((1, PW), lambda i: (0, 0)), pl.BlockSpec((PW, PW), lambda i: (0, 0))],
        out_specs=[blk, pl.BlockSpec((b, NH, DH, DH), lambda i: (0, 0, 0, 0))],
        out_shape=[jax.ShapeDtypeStruct((b, l, W), F32), jax.ShapeDtypeStruct((b, NH, DH, DH), F32)],
        scratch_shapes=[pltpu.VMEM((b, NPAIR, PW, PW), F32)],
        compiler_params=_cp("arbitrary"), name="gdn_scan",
    )(uv, w, qk, qd, kdt, egl, z, nw2, ones_pair)


SB_TQ = 256
SB_TK = 128


def _sb_block(z, r, u_ref, mask):
    sp = _softplus(z)
    ln = -sp
    if mask is not None:
        ln = jnp.where(mask, ln, 0.0)
    e = _dot_exact_rhs(ln, u_ref[...])
    w = jnp.exp((z - sp) + e + r)
    if mask is not None:
        w = jnp.where(mask, w, 0.0)
    return w, r + jnp.sum(ln, axis=-1, keepdims=True)


def _sb_prompt_kernel(q_ref, k_ref, vt_ref, a_ref, o_ref, qbd_ref, nz0, nz1, d0, d1, w0, w1, e0, e1, acc_ref):
    i = pl.program_id(1)
    tq, tk = SB_TQ, SB_TK
    nzb, db, wb, eb = (nz0, nz1), (d0, d1), (w0, w1), (e0, e1)
    tiles = [slice(h * tq, (h + 1) * tq) for h in range(NH)]
    qt = (q_ref[...] * (-(DH ** -0.5) * LOG2E)).T
    rowh = lax.broadcasted_iota(jnp.int32, (W, tq), 0) // DH
    for h, t in enumerate(tiles):
        qbd_ref[:, t] = jnp.where(rowh == h, qt, 0.0).astype(BF16)
    kk = lax.broadcasted_iota(jnp.int32, (tk, tq), 0)
    qq = lax.broadcasted_iota(jnp.int32, (tk, tq), 1)

    def stage_z(j, dst):
        kb = k_ref[pl.ds(pl.multiple_of(jnp.maximum(j, 0) * tk, tk), tk), :]
        for t in tiles:
            dst[:, t] = _dot(kb, qbd_ref[:, t])

    def stage_l(src, dst, erow, mask=None):
        for t in tiles:
            nz = src[:, t]
            ln = jnp.minimum(nz, 0.0) - jnp.log2(1.0 + jnp.exp2(-jnp.abs(nz)))
            if mask is not None:
                ln = jnp.where(mask, ln, 0.0)
            et = _dot(a_ref[...], ln.astype(BF16))
            dst[:, t] = et - nz
            erow[:, t] = et[0:1, :]

    def stage_x(src, erow, dst, r, mask=None):
        for t in tiles:
            w = jnp.exp2(src[:, t] + r[:, t])
            if mask is not None:
                w = jnp.where(mask, w, 0.0)
            dst[:, t] = w.astype(BF16)
        return r + erow[...]

    def stage_p(j, src):
        vt = vt_ref[:, pl.ds(pl.multiple_of(j * tk, tk), tk)]
        for h, t in enumerate(tiles):
            acc_ref[h * DH:(h + 1) * DH, :] += _dot(vt[h * DH:(h + 1) * DH, :], src[:, t])

    assert tq == 2 * tk
    acc_ref[...] = jnp.zeros(acc_ref.shape, F32)
    r = jnp.zeros((1, NH * tq), F32)
    jb = (2 * i + 1, 2 * i)
    causal = [(j * tk + kk) < (i * tq + qq) for j in jb]
    for s in range(2):
        stage_z(jb[s], nzb[s])
    for s in range(2):
        stage_l(nzb[s], db[s], eb[s], causal[s])
    for s in range(2):
        r = stage_x(db[s], eb[s], wb[s], r, causal[s])
    for s in range(2):
        stage_p(jb[s], wb[s])
    j0 = 2 * i - 1
    stage_z(j0, nzb[1])
    stage_z(j0 - 1, nzb[0])
    stage_l(nzb[1], db[1], eb[1])
    stage_l(nzb[0], db[0], eb[0])
    r = stage_x(db[1], eb[1], wb[1], r)
    stage_z(j0 - 2, nzb[1])

    def tick(j, p, r):
        stage_p(j, wb[p])
        stage_z(j - 3, nzb[1 - p])
        stage_l(nzb[p], db[p], eb[p])
        return stage_x(db[1 - p], eb[1 - p], wb[1 - p], r)

    def body(t, r):
        j = j0 - 2 * t
        return tick(j - 1, 0, tick(j, 1, r))

    lax.fori_loop(0, i, body, r)
    o_ref[...] = acc_ref[...].T


def _sb_prompt(q, k16, vt16, a_incl):
    b, l, _ = q.shape
    wide = (SB_TK, NH * SB_TQ)
    return pl.pallas_call(
        _sb_prompt_kernel, grid=(b, l // SB_TQ),
        in_specs=[_tok_spec(SB_TQ, W),
                  pl.BlockSpec((None, l, W), lambda b, i: (b, 0, 0)),
                  pl.BlockSpec((None, W, l), lambda b, i: (b, 0, 0)),
                  _full_spec((SB_TK, SB_TK))],
        out_specs=_tok_spec(SB_TQ, W),
        out_shape=jax.ShapeDtypeStruct((b, l, W), F32),
        scratch_shapes=[pltpu.VMEM((W, NH * SB_TQ), BF16)] + [pltpu.VMEM(wide, F32)] * 4 +
                       [pltpu.VMEM(wide, BF16)] * 2 + [pltpu.VMEM((1, NH * SB_TQ), F32)] * 2 +
                       [pltpu.VMEM((W, SB_TQ), F32)],
        compiler_params=_cp("arbitrary", "arbitrary"), name="sb_prompt",
    )(q, k16, vt16, a_incl)


def _strict_upper(n):
    i = jnp.arange(n)
    return (i[:, None] > i[None, :]).astype(BF16)


def _convprep_sample_kernel(c3_ref, qkv_ref, ba_ref, cb0_ref, cd0_ref, wb_ref, wd_ref, gp_ref, ones_ref,
                            yb_ref, qt_ref, kt_ref, vt_ref, gbt_ref, cbs_ref, cds_ref):
    nt = c3_ref.shape[0]
    xb = [cb0_ref[0], cb0_ref[1]]
    xd = [cd0_ref[0], cd0_ref[1], cd0_ref[2]]
    for t in range(nt):
        c3 = c3_ref[t]
        xb.append(c3[:, W:2 * W] * c3[:, 2 * W:])
        xd.append(qkv_ref[t])
    for t in range(nt):
        conv = xb[t] * wb_ref[0:1, :] + xb[t + 1] * wb_ref[1:2, :] + xb[t + 2] * wb_ref[2:3, :]
        yb_ref[t] = c3_ref[t][:, :W] * conv
        conv = (xd[t] * wd_ref[0:1, :] + xd[t + 1] * wd_ref[1:2, :] + xd[t + 2] * wd_ref[2:3, :]
                + xd[t + 3] * wd_ref[3:4, :])
        q, k, v, gb = _delta_act(conv, ba_ref[t], gp_ref, ones_ref)
        qt_ref[t] = q.T
        kt_ref[t] = k.T
        vt_ref[t] = v.T
        gbt_ref[t] = gb.T[0:8, :]
    cbs_ref[0] = xb[nt]
    cbs_ref[1] = xb[nt + 1]
    for j in range(3):
        cds_ref[j] = xd[nt + j]


def _convprep_sample(c3, qkv, ba, cb0, cd0, wb, wd, gp, ones_bd):
    t, s, _ = c3.shape
    sds = jax.ShapeDtypeStruct
    return pl.pallas_call(
        _convprep_sample_kernel,
        out_shape=[sds((t, s, W), F32), sds((t, W, s), F32), sds((t, W, s), F32), sds((t, W, s), F32),
                   sds((t, 8, s), F32), sds((2, s, W), F32), sds((3, s, 3 * W), F32)],
        compiler_params=pltpu.CompilerParams(vmem_limit_bytes=VMEM_LIMIT), name="convprep_sample",
    )(c3, qkv, ba, cb0, cd0, wb, wd, gp, ones_bd)


def _gdn_sample_kernel(qt_ref, kt_ref, vt_ref, gbt_ref, s0_ref, ot_ref, s1_ref):
    h = pl.program_id(0)
    nt = qt_ref.shape[0]
    ns = qt_ref.shape[2]
    for t in range(nt):
        src = s0_ref if t == 0 else s1_ref
        b = gbt_ref[t, pl.ds(h, 1), :]
        a = jnp.exp(gbt_ref[t, pl.ds(NH + h, 1), :])

        def ks_body(i, acc):
            return acc + kt_ref[t, pl.ds(i, 1), :] * src[i]

        ks = lax.fori_loop(0, DH, ks_body, jnp.zeros((DH, ns), F32), unroll=8)
        u = b * (vt_ref[t] - a * ks)

        def up_body(i, acc):
            sn = a * src[i] + kt_ref[t, pl.ds(i, 1), :] * u
            s1_ref[i] = sn
            return acc + qt_ref[t, pl.ds(i, 1), :] * sn

        ot_ref[t] = lax.fori_loop(0, DH, up_body, jnp.zeros((DH, ns), F32), unroll=8)


def _gdn_sample(qt, kt, vt, gbt, s0):
    t, _, s = qt.shape
    hb = pl.BlockSpec((t, DH, s), lambda h: (0, h, 0))
    sb = pl.BlockSpec((None, DH, DH, s), lambda h: (h, 0, 0, 0))
    return pl.pallas_call(
        _gdn_sample_kernel, grid=(NH,),
        in_specs=[hb, hb, hb, pl.BlockSpec((t, 8, s), lambda h: (0, 0, 0)), sb],
        out_specs=[hb, sb],
        out_shape=[jax.ShapeDtypeStruct((t, W, s), F32), jax.ShapeDtypeStruct((NH, DH, DH, s), F32)],
        compiler_params=_cp("arbitrary"), name="gdn_sample",
    )(qt, kt, vt, gbt, s0)


def _gdn_post_sample_kernel(ot_ref, z_ref, nw_ref, ones_ref, y_ref):
    for t in range(ot_ref.shape[0]):
        o = ot_ref[t].T
        ms = _head_sumsq(o, ones_ref) * (1.0 / DH)
        y_ref[t] = o * lax.rsqrt(ms + RMS_EPS) * nw_ref[...] * _silu(z_ref[t])


def _gdn_post_sample(ot, z, nw4, ones_bd):
    t, _, s = ot.shape
    return pl.pallas_call(
        _gdn_post_sample_kernel, out_shape=jax.ShapeDtypeStruct((t, s, W), F32), name="gdn_post_sample",
    )(ot, z, nw4, ones_bd)


def _sb_sample_kernel(pt_ref, q_ref, kn_ref, vn_ref, *rest):
    n_pages = (len(rest) - 4) // 2
    kp = rest[:n_pages]
    vp = rest[n_pages:2 * n_pages]
    u_ref, o_ref, kpad, vpad = rest[2 * n_pages:]
    nt = q_ref.shape[0]
    nq = NH * nt
    row = lax.broadcasted_iota(jnp.int32, (nq, W), 0)
    lane = lax.broadcasted_iota(jnp.int32, (nq, W), 1)
    q = q_ref[...] * (DH ** -0.5)
    qbd = jnp.where(row // nt == lane // DH, jnp.concatenate([q] * NH, axis=0), 0.0).astype(BF16)

    kpad[...] = jnp.zeros(kpad.shape, F32)
    vpad[...] = jnp.zeros(vpad.shape, F32)
    kpad[0:nt, :] = kn_ref[...]
    vpad[0:nt, :] = vn_ref[...]
    qi = lax.broadcasted_iota(jnp.int32, (nq, PAGE), 0) % nt
    ki = lax.broadcasted_iota(jnp.int32, (nq, PAGE), 1)
    z = _dot_nt(qbd, kpad[...].astype(BF16))
    w, r = _sb_block(z, jnp.zeros((nq, 1), F32), u_ref, ki < qi)
    acc = _dot(w.astype(BF16), vpad[...].astype(BF16))

    pages = list(range(n_pages - 1, -1, -1))
    z = [_dot(qbd, kp[p][...].astype(BF16)) for p in pages]
    sp = _each(_softplus, z)
    e = _each(lambda s: _dot_exact_rhs(-s, u_ref[...]), sp)
    for p, zp, spp, ep in zip(pages, z, sp, e):
        w = jnp.exp((zp - spp) + ep + r)
        acc = acc + _dot_nt(w.astype(BF16), vp[p][...].astype(BF16))
        r = r - jnp.sum(spp, axis=-1, keepdims=True)

    sel = jnp.where(row // nt == lane // DH, acc, 0.0)
    out = sel[0:nt]
    for h in range(1, NH):
        out = out + sel[h * nt:(h + 1) * nt]
    o_ref[...] = out


def _sb_sample(page_table, q, kn, vn, cache_kt, cache_vt, layer, u_strict):
    s, t, _ = q.shape
    n_pages = page_table.shape[1]
    seq = pl.BlockSpec((None, t, W), lambda s_, pt: (s_, 0, 0))

    def page_spec(p):
        return pl.BlockSpec((None, None, W, PAGE), lambda s_, pt: (layer, pt[s_, p], 0, 0))

    grid_spec = pltpu.PrefetchScalarGridSpec(
        num_scalar_prefetch=1, grid=(s,),
        in_specs=[seq, seq, seq] + [page_spec(p) for p in range(n_pages)] * 2 +
                 [pl.BlockSpec((PAGE, PAGE), lambda s_, pt: (0, 0))],
        out_specs=seq,
        scratch_shapes=[pltpu.VMEM((PAGE, W), F32), pltpu.VMEM((PAGE, W), F32)])
    return pl.pallas_call(
        _sb_sample_kernel, grid_spec=grid_spec,
        out_shape=jax.ShapeDtypeStruct((s, t, W), F32),
        compiler_params=_cp("arbitrary"), name="sb_sample",
    )(page_table, q, kn, vn, *([cache_kt] * n_pages), *([cache_vt] * n_pages), u_strict)


def _layer_weights(l, w_in, ssm_a_re, ssm_a_im, ssm_log_dt, ssm_b_re, ssm_b_im, ssm_c_re, ssm_c_im, ssm_d, w_glu,
                   conv_b_w, delta_conv_w, delta_a_log, delta_dt_bias, delta_norm_w, w_branch, w_gate, w_o,
                   ln1_g, ln1_b, w_ffn_up, w_ffn_down, ln2_g, ln2_b):
    w = w_in[l]
    w_in_p = jnp.concatenate([w[:, :2048], jnp.pad(w[:, 2048:2056], ((0, 0), (0, 120))), w[:, 2056:]],
                             axis=1).astype(BF16)
    apow, bbd = _s5_prep(ssm_a_re[l], ssm_a_im[l], ssm_log_dt[l], ssm_b_re[l], ssm_b_im[l])
    gp = jnp.zeros((2, 128), F32)
    gp = gp.at[0, NH:2 * NH].set(delta_a_log[l]).at[1, NH:2 * NH].set(delta_dt_bias[l])
    hid = jnp.arange(W) // DH
    return dict(
        w_in=w_in_p, apow=apow, bbd=bbd.astype(BF16), cre=_s5_cmat(ssm_c_re[l]).astype(BF16),
        cim=_s5_cmat(ssm_c_im[l]).astype(BF16), d=ssm_d[l].reshape(1, W), wglu=w_glu[l].astype(BF16),
        wb=conv_b_w[l], wd=delta_conv_w[l], gp=gp, nw=delta_norm_w[l].reshape(1, DH),
        nw4=jnp.tile(delta_norm_w[l], NH).reshape(1, W),
        ones_bd=(hid[:, None] == hid[None, :]).astype(BF16),
        wbr=w_branch[l].astype(BF16), wg=w_gate[l].astype(BF16), wo=w_o[l].astype(BF16),
        ln1g=ln1_g[l].reshape(1, D), ln1b=ln1_b[l].reshape(1, D),
        wu=w_ffn_up[l].astype(BF16), wdn=w_ffn_down[l].astype(BF16),
        ln2g=ln2_g[l].reshape(1, D), ln2b=ln2_b[l].reshape(1, D))


def _prompt_layer(x, mod, p, u_strict):
    b, l, _ = x.shape
    u, c3, qkv, z, ba, q, kt, vt, k16, vt16 = _proj(x, mod, p["w_in"], 512, True)
    ya, hre, him = _s5_prompt(u, p["apow"], p["bbd"], p["cre"], p["cim"], p["d"], p["wglu"])
    yb, qd, kd, vd, gb, cbs, cds = _convprep_prompt(c3, qkv, ba, p["wb"], p["wd"], p["gp"], p["ones_bd"])
    uv, w, qk, qdec, kdt, egl = _gdn_local(qd, kd, vd, gb)
    yc, s_new = _gdn_scan(uv, w, qk, qdec, kdt, egl, z, p["nw4"][:, :PW], p["ones_bd"][:PW, :PW])
    idx = jnp.arange(SB_TK)
    a_incl = (idx[None, :] >= idx[:, None]).astype(BF16)
    yd = _sb_prompt(q, k16, vt16, a_incl)
    x1 = _merge(x, mod, ya, yb, yc, yd, p["wg"], p["wbr"], p["wo"], p["ln1g"], p["ln1b"], 256)
    x2 = _ffn(x1, mod, p["wu"], p["wdn"], p["ln2g"], p["ln2b"], 512)
    unt = lambda t: t.reshape(b, NH, DH, l).transpose(0, 3, 1, 2)
    new = (unt(kt), unt(vt), hre.reshape(b, G, P), him.reshape(b, G, P),
           cbs[:, 6:8], s_new, cds[:, 5:8])
    return x2, new


def _sample_layer(x, mod, p, st, page_table, cache_k, cache_v, layer, u_strict):
    h0re, h0im, cb0, s0, cd0 = st
    s = h0re.shape[0]
    n = x.shape[1]
    t = n // s
    u, c3, qkv, z, ba, q, k, v = _proj(x, mod, p["w_in"], n, False)
    tm = lambda a: a.reshape(t, s, a.shape[-1])
    sm = lambda a: tm(a).transpose(1, 0, 2)
    ya, hre, him = _s5_sample(tm(u), h0re.reshape(s, NS), h0im.reshape(s, NS), p["apow"], p["bbd"], p["cre"],
                              p["cim"], p["d"], p["wglu"])
    yb, qt, kt, vt, gbt, cbs, cds = _convprep_sample(tm(c3), tm(qkv), tm(ba), cb0.transpose(1, 0, 2),
                                                      cd0.transpose(1, 0, 2), p["wb"], p["wd"], p["gp"],
                                                      p["ones_bd"])
    ot, s1 = _gdn_sample(qt, kt, vt, gbt, s0.transpose(1, 2, 3, 0))
    yc = _gdn_post_sample(ot, tm(z), p["nw4"], p["ones_bd"])
    ksm, vsm = sm(k), sm(v)
    yd = _sb_sample(page_table, sm(q), ksm, vsm, cache_k, cache_v, layer, u_strict).transpose(1, 0, 2)
    flat = lambda a: a.reshape(1, n, W)
    x1 = _merge(x, mod, flat(ya), flat(yb), flat(yc), flat(yd), p["wg"], p["wbr"], p["wo"], p["ln1g"], p["ln1b"],
                min(256, n))
    x2 = _ffn(x1, mod, p["wu"], p["wdn"], p["ln2g"], p["ln2b"], n)
    new = (ksm.reshape(s, t, NH, DH), vsm.reshape(s, t, NH, DH), hre.reshape(s, G, P), him.reshape(s, G, P),
           cbs.transpose(1, 0, 2), s1.transpose(3, 0, 1, 2), cds.transpose(1, 0, 2))
    return x2, new


def kernel(x_prompt, x_sample, cache_k, cache_v, state_ssm_re, state_ssm_im, state_conv_b, state_delta,
           state_conv_delta, page_table, c_prompt, c_sample, w_ada, b_ada, w_in, ssm_a_re, ssm_a_im, ssm_log_dt,
           ssm_b_re, ssm_b_im, ssm_c_re, ssm_c_im, ssm_d, w_glu, conv_b_w, delta_conv_w, delta_a_log,
           delta_dt_bias, delta_norm_w, w_branch, w_gate, w_o, ln1_g, ln1_b, w_ffn_up, w_ffn_down, ln2_g, ln2_b):
    depth = w_ada.shape[0]
    bp, lp, _ = x_prompt.shape
    bs, ts, _ = x_sample.shape
    n_pool = cache_k.shape[1]

    c_all = jnp.concatenate([c_prompt, c_sample], axis=0)
    pad = (-c_all.shape[0]) % 8
    mod = _mod(jnp.pad(c_all, ((0, pad), (0, 0))), w_ada, b_ada)
    mod_p = mod[:, :bp].reshape(depth, bp, 6, 1, D)
    mod_s = mod[:, bp:bp + bs].reshape(depth, bs, 6, D).transpose(0, 2, 1, 3)
    mod_s = jnp.broadcast_to(mod_s[:, :, None], (depth, 6, ts, bs, D)).reshape(depth, 1, 6, ts * bs, D)

    ck = cache_k.transpose(0, 1, 3, 4, 2).reshape(depth, n_pool, W, PAGE)
    cv = cache_v.transpose(0, 1, 3, 4, 2).reshape(depth, n_pool, W, PAGE)
    u_strict = _strict_upper(PAGE)

    x_p = x_prompt
    x_s = x_sample.transpose(1, 0, 2).reshape(1, ts * bs, D)
    new_p, new_s = [], []
    for l in range(depth):
        p = _layer_weights(l, w_in, ssm_a_re, ssm_a_im, ssm_log_dt, ssm_b_re, ssm_b_im, ssm_c_re, ssm_c_im, ssm_d,
                           w_glu, conv_b_w, delta_conv_w, delta_a_log, delta_dt_bias, delta_norm_w, w_branch,
                           w_gate, w_o, ln1_g, ln1_b, w_ffn_up, w_ffn_down, ln2_g, ln2_b)
        x_p, np_ = _prompt_layer(x_p, mod_p[l], p, u_strict)
        new_p.append(np_)
        st = (state_ssm_re[l], state_ssm_im[l], state_conv_b[l], state_delta[l], state_conv_delta[l])
        x_s, ns_ = _sample_layer(x_s, mod_s[l], p, st, page_table, ck, cv, l, u_strict)
        new_s.append(ns_)
    k_p, v_p, re_p, im_p, cb_p, d_p, cd_p = [jnp.stack(t, axis=0) for t in zip(*new_p)]
    k_s, v_s, re_s, im_s, cb_s, d_s, cd_s = [jnp.stack(t, axis=0) for t in zip(*new_s)]
    y_s = x_s.reshape(ts, bs, D).transpose(1, 0, 2)
    return (x_p, y_s, k_p, v_p, k_s, v_s, re_p, im_p, re_s, im_s, cb_p, cb_s, d_p, d_s, cd_p, cd_s)
```

```python
import jax
import jax.numpy as jnp
from jax import lax
from jax.experimental import pallas as pl
from jax.experimental.pallas import tpu as pltpu

F32 = jnp.float32
BF16 = jnp.bfloat16

D = 1024
W = 256
NH = 4
DH = 64
G = 16
GW = 16
P = 64
NS = G * P
DFF = 2816
CHUNK = 64
PAGE = 128
ALPHA = (2 * 2) ** 0.25
LN_EPS = 1e-5
RMS_EPS = 1e-6
LOG2E = 1.4426950408889634
VMEM_LIMIT = 56 * 1024 * 1024

SEGS = (("ssm", 256, 0, 256), ("conv", 768, 256, 768), ("qkv", 768, 1024, 768), ("z", 256, 1792, 256),
        ("ba", 128, 2048, 8), ("q", 256, 2056, 256), ("k", 256, 2312, 256), ("v", 256, 2568, 256))
NP_IN = sum(s[1] for s in SEGS)


def _cp(*sem):
    return pltpu.CompilerParams(dimension_semantics=sem, vmem_limit_bytes=VMEM_LIMIT)


def _dot(a, b):
    return jnp.dot(a, b, preferred_element_type=F32)


def _dot_nt(a, b):
    return lax.dot_general(a, b, (((1,), (1,)), ((), ())), preferred_element_type=F32)


def _split(x):
    hi = x.astype(BF16)
    lo = (x - hi.astype(F32)).astype(BF16)
    return hi, lo


def _dot_exact_rhs(x, m_bf16):
    hi, lo = _split(x)
    return _dot(hi, m_bf16) + _dot(lo, m_bf16)


def _mm3(a, b):
    ah, al = _split(a)
    bh, bl = _split(b)
    return _dot(ah, bh) + (_dot(ah, bl) + _dot(al, bh))


def _mm3_nt(a, b):
    ah, al = _split(a)
    bh, bl = _split(b)
    return _dot_nt(ah, bh) + (_dot_nt(ah, bl) + _dot_nt(al, bh))


def _ln(x):
    mu = jnp.mean(x, axis=-1, keepdims=True)
    xc = x - mu
    var = jnp.mean(xc * xc, axis=-1, keepdims=True)
    return xc * lax.rsqrt(var + LN_EPS)


def _softplus(x):
    return jnp.maximum(x, 0.0) + jnp.log1p(jnp.exp(-jnp.abs(x)))


def _silu(x):
    return x * jax.nn.sigmoid(x)


def _mod_kernel(c_ref, w_ref, b_ref, o_ref):
    s = _silu(c_ref[...]).astype(BF16)
    o_ref[0] = _dot(s, w_ref[0].astype(BF16)) + b_ref[0]


def _mod(c_all, w_ada, b_ada):
    mp = c_all.shape[0]
    depth = w_ada.shape[0]
    tn = 1024
    return pl.pallas_call(
        _mod_kernel, grid=(depth, 6 * D // tn),
        in_specs=[pl.BlockSpec((mp, D), lambda l, j: (0, 0)),
                  pl.BlockSpec((1, D, tn), lambda l, j: (l, 0, j)),
                  pl.BlockSpec((1, 1, tn), lambda l, j: (l, 0, j))],
        out_specs=pl.BlockSpec((1, mp, tn), lambda l, j: (l, 0, j)),
        out_shape=jax.ShapeDtypeStruct((depth, mp, 6 * D), F32),
        compiler_params=_cp("arbitrary", "arbitrary"), name="ada_mod",
    )(c_all, w_ada, b_ada.reshape(depth, 1, 6 * D))


def _mod_spec(mod, tm):
    if mod.shape[2] == 1:
        return pl.BlockSpec((None, 6, 1, D), lambda b, i: (b, 0, 0, 0))
    return pl.BlockSpec((None, 6, tm, D), lambda b, i: (b, 0, i, 0))


def _tok_spec(tm, c):
    return pl.BlockSpec((None, tm, c), lambda b, i: (b, i, 0))


def _full_spec(shape):
    n = len(shape)
    return pl.BlockSpec(shape, lambda b, i: (0,) * n)


def _proj_kernel(x_ref, mod_ref, w_ref, *outs):
    h = (_ln(x_ref[...]) * (1.0 + mod_ref[1]) + mod_ref[0]).astype(BF16)
    off = 0
    transposed_kv = len(outs) > len(SEGS)
    for o_ref, seg in zip(outs, SEGS):
        y = _dot(h, w_ref[:, off:off + seg[1]])
        off += seg[1]
        if transposed_kv and seg[0] in ("k", "v"):
            yt = y.T
            o_ref[...] = yt
            if seg[0] == "k":
                outs[len(SEGS)][...] = y.astype(BF16)
            else:
                outs[len(SEGS) + 1][...] = yt.astype(BF16)
        else:
            o_ref[...] = y


def _proj(x, mod, w_in_p, tm, transposed_kv):
    b, l, _ = x.shape
    out_specs = [_tok_spec(tm, s[1]) for s in SEGS]
    out_shape = [jax.ShapeDtypeStruct((b, l, s[1]), F32) for s in SEGS]
    if transposed_kv:
        tspec = pl.BlockSpec((None, W, tm), lambda b, i: (b, 0, i))
        out_specs = out_specs[:-2] + [tspec, tspec, _tok_spec(tm, W), tspec]
        out_shape = out_shape[:-2] + [jax.ShapeDtypeStruct((b, W, l), F32)] * 2 + \
            [jax.ShapeDtypeStruct((b, l, W), BF16), jax.ShapeDtypeStruct((b, W, l), BF16)]
    return pl.pallas_call(
        _proj_kernel, grid=(b, l // tm),
        in_specs=[_tok_spec(tm, D), _mod_spec(mod, tm), _full_spec((D, NP_IN))],
        out_specs=out_specs, out_shape=out_shape,
        compiler_params=_cp("arbitrary", "arbitrary"), name="in_proj",
    )(x, mod, w_in_p)


def _merge_kernel(x_ref, mod_ref, ya_ref, yb_ref, yc_ref, yd_ref, wg_ref, wb_ref, wo_ref, lg_ref, lb_ref, o_ref):
    x = x_ref[...]
    h = (_ln(x) * (1.0 + mod_ref[1]) + mod_ref[0]).astype(BF16)
    acc = None
    for n, y_ref in enumerate((ya_ref, yb_ref, yc_ref, yd_ref)):
        gate = jax.nn.sigmoid(_dot(h, wg_ref[:, n * D:(n + 1) * D]))
        br = _dot(y_ref[...].astype(BF16), wb_ref[n])
        acc = gate * br if acc is None else acc + gate * br
    mixed = _dot(acc.astype(BF16), wo_ref[...])
    r = ALPHA * x + (1.0 + mod_ref[2]) * mixed
    o_ref[...] = _ln(r) * lg_ref[...] + lb_ref[...]


def _merge(x, mod, ya, yb, yc, yd, wg, wb, wo, lg, lb, tm):
    b, l, _ = x.shape
    return pl.pallas_call(
        _merge_kernel, grid=(b, l // tm),
        in_specs=[_tok_spec(tm, D), _mod_spec(mod, tm)] + [_tok_spec(tm, W)] * 4 +
                 [_full_spec((D, 4 * D)), _full_spec((4, W, D)), _full_spec((D, D)),
                  _full_spec((1, D)), _full_spec((1, D))],
        out_specs=_tok_spec(tm, D),
        out_shape=jax.ShapeDtypeStruct((b, l, D), F32),
        compiler_params=_cp("arbitrary", "arbitrary"), name="merge",
    )(x, mod, ya, yb, yc, yd, wg, wb, wo, lg, lb)


FF_CHUNK = 256


def _ffn_kernel(x_ref, mod_ref, wu_ref, wd_ref, lg_ref, lb_ref, o_ref):
    x = x_ref[...]
    h = (_ln(x) * (1.0 + mod_ref[4]) + mod_ref[3]).astype(BF16)
    acc = None
    for c in range(DFF // FF_CHUNK):
        lo = c * FF_CHUNK
        up_a = _dot(h, wu_ref[:, lo:lo + FF_CHUNK])
        up_b = _dot(h, wu_ref[:, DFF + lo:DFF + lo + FF_CHUNK])
        t = (_silu(up_a) * up_b).astype(BF16)
        d = _dot(t, wd_ref[lo:lo + FF_CHUNK, :])
        acc = d if acc is None else acc + d
    r = ALPHA * x + (1.0 + mod_ref[5]) * acc
    o_ref[...] = _ln(r) * lg_ref[...] + lb_ref[...]


def _ffn(x, mod, wu, wd, lg, lb, tm):
    b, l, _ = x.shape
    return pl.pallas_call(
        _ffn_kernel, grid=(b, l // tm),
        in_specs=[_tok_spec(tm, D), _mod_spec(mod, tm), _full_spec((D, 2 * DFF)), _full_spec((DFF, D)),
                  _full_spec((1, D)), _full_spec((1, D))],
        out_specs=_tok_spec(tm, D),
        out_shape=jax.ShapeDtypeStruct((b, l, D), F32),
        compiler_params=_cp("arbitrary", "arbitrary"), name="ffn",
    )(x, mod, wu, wd, lg, lb)


APOW_LEVELS = 8


def _s5_prep_kernel(are_ref, aim_ref, ldt_ref, bre_ref, bim_ref, apow_ref, bbre_ref, bbim_ref):
    a_re = are_ref[...]
    a_im = aim_ref[...]
    dt = jnp.exp(ldt_ref[...])
    mag = jnp.exp(dt * a_re)
    ab_re = mag * jnp.cos(dt * a_im)
    ab_im = mag * jnp.sin(dt * a_im)
    den = a_re * a_re + a_im * a_im
    f_re = ((ab_re - 1.0) * a_re + ab_im * a_im) / den
    f_im = (ab_im * a_re - (ab_re - 1.0) * a_im) / den
    b_re = bre_ref[...]
    b_im = bim_ref[...]
    bbre_ref[...] = f_re * b_re - f_im * b_im
    bbim_ref[...] = f_re * b_im + f_im * b_re
    pr, pi = ab_re, ab_im
    for k in range(APOW_LEVELS):
        apow_ref[2 * k] = pr
        apow_ref[2 * k + 1] = pi
        pr, pi = pr * pr - pi * pi, 2.0 * pr * pi


def _s5_prep(a_re, a_im, log_dt, b_re, b_im):
    rep = lambda t: jnp.repeat(t, GW, axis=0)
    ldt = jnp.broadcast_to(log_dt[:, None], (G, P))
    bt = lambda t: jnp.transpose(t, (0, 2, 1)).reshape(G * GW, P)
    apow_x, bb_re, bb_im = pl.pallas_call(
        _s5_prep_kernel,
        out_shape=[jax.ShapeDtypeStruct((2 * APOW_LEVELS, G * GW, P), F32), jax.ShapeDtypeStruct((G * GW, P), F32),
                   jax.ShapeDtypeStruct((G * GW, P), F32)], name="s5_prep",
    )(rep(a_re), rep(a_im), rep(ldt), bt(b_re), bt(b_im))
    apow = apow_x[:, ::GW, :].reshape(2 * APOW_LEVELS, NS)
    eye = jnp.eye(G, dtype=F32)
    bd = lambda t: jnp.einsum("ghp,gk->ghkp", t.reshape(G, GW, P), eye).reshape(W, NS)
    return apow, jnp.concatenate([bd(bb_re), bd(bb_im)], axis=1)


def _s5_cmat(c):
    eye = jnp.eye(G, dtype=F32)
    return jnp.einsum("ghp,gk->gpkh", c, eye).reshape(NS, W)


def _s5_glu(y_state, u, d_ref, wglu_ref):
    ys = jax.nn.gelu(y_state + d_ref[...] * u)
    return ys * jax.nn.sigmoid(_dot(ys.astype(BF16), wglu_ref[...]))


def _s5_out(hr, hi, u, cre_ref, cim_ref, d_ref, wglu_ref):
    y = _dot(hr.astype(BF16), cre_ref[...]) - _dot(hi.astype(BF16), cim_ref[...])
    return _s5_glu(y, u, d_ref, wglu_ref)


S5_TM = 256
LANES = 128


def _s5_prompt_kernel(u_ref, apow_ref, bbd_ref, cre_ref, cim_ref, d_ref, wglu_ref,
                      ya_ref, hre_ref, him_ref, sbuf, hprev, hbuf):
    tm = S5_TM
    ng = tm // 8
    i = pl.program_id(1)

    @pl.when(i == 0)
    def _():
        hprev[...] = jnp.zeros((8, 2 * NS), F32)

    u = u_ref[...]
    nl = NS // LANES
    bu = _dot(u.astype(BF16), bbd_ref[...])
    for c in range(2 * nl):
        sbuf[c] = bu[:, c * LANES:(c + 1) * LANES]
    row = lax.broadcasted_iota(jnp.int32, (ng, LANES), 0)
    cmul = lambda ar, ai, xr, xi: (ar * xr - ai * xi, ar * xi + ai * xr)
    for c in range(nl):
        cr = slice(c * LANES, (c + 1) * LANES)
        ci = slice(NS + c * LANES, NS + (c + 1) * LANES)
        a1r = apow_ref[0:1, cr]
        a1i = apow_ref[1:2, cr]
        xr = [sbuf[c, pl.ds(s, ng, stride=8), :] for s in range(8)]
        xi = [sbuf[nl + c, pl.ds(s, ng, stride=8), :] for s in range(8)]
        wr, wi = xr[0], xi[0]
        for s in range(1, 8):
            pr, pi = cmul(a1r, a1i, wr, wi)
            wr, wi = pr + xr[s], pi + xi[s]
        gr = jnp.where(row == 0, hprev[0:1, cr], pltpu.roll(wr, 1, 0))
        gi = jnp.where(row == 0, hprev[0:1, ci], pltpu.roll(wi, 1, 0))
        for k, shift in enumerate([1 << b for b in range(ng.bit_length() - 1)]):
            ar = apow_ref[6 + 2 * k:7 + 2 * k, cr]
            ai = apow_ref[7 + 2 * k:8 + 2 * k, cr]
            pr, pi = cmul(ar, ai, jnp.where(row >= shift, pltpu.roll(gr, shift, 0), 0.0),
                          jnp.where(row >= shift, pltpu.roll(gi, shift, 0), 0.0))
            gr, gi = gr + pr, gi + pi
        hr, hi = gr, gi
        for s in range(8):
            pr, pi = cmul(a1r, a1i, hr, hi)
            hr, hi = pr + xr[s], pi + xi[s]
            hbuf[c, pl.ds(s, ng, stride=8), :] = hr
            hbuf[nl + c, pl.ds(s, ng, stride=8), :] = hi
        hprev[0:1, cr] = hr[ng - 1:ng]
        hprev[0:1, ci] = hi[ng - 1:ng]
    h_re = jnp.concatenate([hbuf[c] for c in range(nl)], axis=1)
    h_im = jnp.concatenate([hbuf[nl + c] for c in range(nl)], axis=1)
    ya_ref[...] = _s5_out(h_re, h_im, u, cre_ref, cim_ref, d_ref, wglu_ref)

    @pl.when(i == pl.num_programs(1) - 1)
    def _():
        hre_ref[...] = hprev[0:1, :NS]
        him_ref[...] = hprev[0:1, NS:]


def _s5_prompt(u, apow, bbd, cre, cim, d, wglu):
    b, l, _ = u.shape
    tm = S5_TM
    return pl.pallas_call(
        _s5_prompt_kernel, grid=(b, l // tm),
        in_specs=[_tok_spec(tm, W), _full_spec((2 * APOW_LEVELS, NS)), _full_spec((W, 2 * NS)), _full_spec((NS, W)),
                  _full_spec((NS, W)), _full_spec((1, W)), _full_spec((W, W))],
        out_specs=[_tok_spec(tm, W), pl.BlockSpec((None, 1, NS), lambda b, i: (b, 0, 0)),
                   pl.BlockSpec((None, 1, NS), lambda b, i: (b, 0, 0))],
        out_shape=[jax.ShapeDtypeStruct((b, l, W), F32), jax.ShapeDtypeStruct((b, 1, NS), F32),
                   jax.ShapeDtypeStruct((b, 1, NS), F32)],
        scratch_shapes=[pltpu.VMEM((2 * NS // LANES, tm, LANES), F32), pltpu.VMEM((8, 2 * NS), F32),
                        pltpu.VMEM((2 * NS // LANES, tm, LANES), F32)],
        compiler_params=_cp("arbitrary", "arbitrary"), name="s5_prompt",
    )(u, apow, bbd, cre, cim, d, wglu)


def _s5_sample_kernel(u_ref, h0re_ref, h0im_ref, apow_ref, bbd_ref, cre_ref, cim_ref, d_ref, wglu_ref,
                      ya_ref, hre_ref, him_ref):
    ar = apow_ref[0:1, :]
    ai = apow_ref[1:2, :]
    hr = h0re_ref[...]
    hi = h0im_ref[...]
    for t in range(u_ref.shape[0]):
        u = u_ref[t]
        bu = _dot(u.astype(BF16), bbd_ref[...])
        hr, hi = ar * hr - ai * hi + bu[:, :NS], ar * hi + ai * hr + bu[:, NS:]
        ya_ref[t] = _s5_out(hr, hi, u, cre_ref, cim_ref, d_ref, wglu_ref)
    hre_ref[...] = hr
    him_ref[...] = hi


def _s5_sample(u, h0re, h0im, apow, bbd, cre, cim, d, wglu):
    t, s, _ = u.shape
    return pl.pallas_call(
        _s5_sample_kernel,
        out_shape=[jax.ShapeDtypeStruct((t, s, W), F32), jax.ShapeDtypeStruct((s, NS), F32),
                   jax.ShapeDtypeStruct((s, NS), F32)],
        compiler_params=pltpu.CompilerParams(vmem_limit_bytes=VMEM_LIMIT), name="s5_sample",
    )(u, h0re, h0im, apow, bbd, cre, cim, d, wglu)


def _head_sumsq(x, ones_ref):
    return _dot_exact_rhs(x * x, ones_ref[...])


def _delta_act(conv, ba, gp_ref, ones_ref):
    a = _silu(conv)
    q = a[:, :W]
    k = a[:, W:2 * W]
    v = a[:, 2 * W:]
    q = q * lax.rsqrt(_head_sumsq(q, ones_ref) + RMS_EPS) * (DH ** -0.5)
    k = k * lax.rsqrt(_head_sumsq(k, ones_ref) + RMS_EPS)
    lane = lax.broadcasted_iota(jnp.int32, ba.shape, 1)
    beta = jax.nn.sigmoid(ba)
    g = -jnp.exp(gp_ref[0:1, :]) * _softplus(ba + gp_ref[1:2, :])
    gb = jnp.where(lane < NH, beta, jnp.where(lane < 2 * NH, g, 0.0))
    return q, k, v, gb


CONV_TM = 256


def _convprep_prompt_kernel(c3_ref, qkv_ref, ba_ref, wb_ref, wd_ref, gp_ref, ones_ref,
                            yb_ref, q_ref, k_ref, v_ref, gb_ref, cbs_ref, cds_ref, cbuf, dbuf):
    tm = CONV_TM
    i = pl.program_id(1)

    @pl.when(i == 0)
    def _():
        cbuf[0:8, :] = jnp.zeros((8, W), F32)
        dbuf[0:8, :] = jnp.zeros((8, 3 * W), F32)

    c3 = c3_ref[...]
    cx = c3[:, W:2 * W] * c3[:, 2 * W:]
    cbuf[8:8 + tm, :] = cx
    full = cbuf[...]
    conv = wb_ref[2:3, :] * cx
    for j in (1, 2):
        conv = conv + wb_ref[2 - j:3 - j, :] * pltpu.roll(full, j, 0)[8:]
    yb_ref[...] = c3[:, :W] * conv
    cbs_ref[...] = cbuf[tm:tm + 8, :]
    cbuf[0:8, :] = cbuf[tm:tm + 8, :]

    x = qkv_ref[...]
    dbuf[8:8 + tm, :] = x
    full = dbuf[...]
    conv = wd_ref[3:4, :] * x
    for j in (1, 2, 3):
        conv = conv + wd_ref[3 - j:4 - j, :] * pltpu.roll(full, j, 0)[8:]
    cds_ref[...] = dbuf[tm:tm + 8, :]
    dbuf[0:8, :] = dbuf[tm:tm + 8, :]
    q, k, v, gb = _delta_act(conv, ba_ref[...], gp_ref, ones_ref)
    q_ref[...] = q
    k_ref[...] = k
    v_ref[...] = v
    gb_ref[...] = gb


def _convprep_prompt(c3, qkv, ba, wb, wd, gp, ones_bd):
    b, l, _ = c3.shape
    tm = CONV_TM
    st = lambda c: pl.BlockSpec((None, 8, c), lambda b, i: (b, 0, 0))
    return pl.pallas_call(
        _convprep_prompt_kernel, grid=(b, l // tm),
        in_specs=[_tok_spec(tm, 3 * W), _tok_spec(tm, 3 * W), _tok_spec(tm, 128), _full_spec((3, W)),
                  _full_spec((4, 3 * W)), _full_spec((2, 128)), _full_spec((W, W))],
        out_specs=[_tok_spec(tm, W)] * 4 + [_tok_spec(tm, 128), st(W), st(3 * W)],
        out_shape=[jax.ShapeDtypeStruct((b, l, W), F32)] * 4 +
                  [jax.ShapeDtypeStruct((b, l, 128), F32), jax.ShapeDtypeStruct((b, 8, W), F32),
                   jax.ShapeDtypeStruct((b, 8, 3 * W), F32)],
        scratch_shapes=[pltpu.VMEM((tm + 8, W), F32), pltpu.VMEM((tm + 8, 3 * W), F32)],
        compiler_params=_cp("arbitrary", "arbitrary"), name="convprep_prompt",
    )(c3, qkv, ba, wb, wd, gp, ones_bd)


GDN_CPG = 4
PW = 2 * DH
NPAIR = NH // 2


def _each(fn, *lists):
    return [fn(*args) for args in zip(*lists)]


def _pair_diag(x, blk):
    return jnp.where(blk, jnp.concatenate([x, x], axis=0), 0.0)


def _pair_fold(x):
    n = x.shape[0] // 2
    return x[:n] + x[n:]


def _tri_inv_all(ms, ii, jj):
    eye = (ii == jj).astype(F32)
    blk = (ii // 16) == (jj // 16)
    nd = _each(lambda m: jnp.where(blk, -m, 0.0), ms)
    e = _each(lambda m: jnp.where(blk, 0.0, m), ms)
    n2 = _each(lambda a: _mm3(a, a), nd)
    n4 = _each(lambda a: _mm3(a, a), n2)
    n8 = _each(lambda a: _mm3(a, a), n4)
    d = _each(lambda a, b: _mm3(eye + a, eye + b), nd, n2)
    d = _each(lambda a, b: _mm3(a, eye + b), d, n4)
    dinv = _each(lambda a, b: _mm3(a, eye + b), d, n8)
    x = _each(_mm3, dinv, e)
    x2 = _each(lambda a: _mm3(a, a), x)
    t = _each(lambda a, b: _mm3(eye - a, eye + b), x, x2)
    return _each(_mm3, t, dinv)


def _gdn_local_kernel(q_ref, k_ref, v_ref, gb_ref, uv_ref, w_ref, qk_ref, qd_ref, kdt_ref, egl_ref):
    cs = CHUNK
    n = 2 * cs
    i64 = lax.broadcasted_iota(jnp.int32, (cs, cs), 0)
    j64 = lax.broadcasted_iota(jnp.int32, (cs, cs), 1)
    tri_incl = (i64 >= j64).astype(BF16)
    tri_up = (i64 <= j64).astype(BF16)
    ii = lax.broadcasted_iota(jnp.int32, (n, n), 0)
    jj = lax.broadcasted_iota(jnp.int32, (n, n), 1)
    blk = (ii // cs) == (jj // cs)
    incl = jnp.logical_and(blk, ii >= jj)
    strict = jnp.logical_and(blk, ii > jj)
    top = lax.broadcasted_iota(jnp.int32, (n, 1), 0) < cs
    first = lax.broadcasted_iota(jnp.int32, (cs, PW), 1) < DH
    rows = [slice(c * cs, (c + 1) * cs) for c in range(GDN_CPG)]
    gbc = [gb_ref[r, :] for r in rows]
    gc_all = _each(lambda g: sum(_dot(tri_incl, p) for p in _split(g)), gbc)
    gct_all = _each(lambda g: _dot_exact_rhs(g.T, tri_up), gbc)
    chains = [(c, p) for c in range(GDN_CPG) for p in range(NPAIR)]
    psl = lambda p: slice(p * PW, (p + 1) * PW)
    col2 = lambda m, i: jnp.concatenate([m[:, i:i + 1], m[:, i + 1:i + 2]], axis=0)
    gcol = [col2(gc_all[c], NH + 2 * p) for c, p in chains]
    beta = [col2(gbc[c], 2 * p) for c, p in chains]
    grow = [jnp.concatenate([gct_all[c][NH + 2 * p:NH + 2 * p + 1, :], gct_all[c][NH + 2 * p + 1:NH + 2 * p + 2, :]],
                            axis=1) for c, p in chains]
    qs = [q_ref[rows[c], psl(p)] for c, p in chains]
    qp = [_pair_diag(x, blk) for x in qs]
    kp = [_pair_diag(k_ref[rows[c], psl(p)], blk) for c, p in chains]
    vp = [_pair_diag(v_ref[rows[c], psl(p)], blk) for c, p in chains]
    e = _each(lambda a, b: jnp.exp(jnp.where(incl, a - b, 0.0)), gcol, grow)
    gram = _each(lambda q, k: _mm3_nt(jnp.concatenate([q, k], axis=0), k), qp, kp)
    m = _each(lambda b, g, d: b * g[n:] * jnp.where(strict, d, 0.0), beta, gram, e)
    egc = _each(jnp.exp, gcol)
    rhs = _each(lambda b, v, g, k: jnp.concatenate([b * v, (b * g) * k], axis=1), beta, vp, egc, kp)
    sol = _each(_mm3, _tri_inv_all(m, ii, jj), rhs)
    qk = _each(lambda g, d: _pair_fold(g[:n] * jnp.where(incl, d, 0.0)), gram, e)
    glast = [jnp.where(top, g[cs - 1:cs, :], g[n - 1:n, :]) for g in gcol]
    kdt = _each(lambda k, gl, g: _pair_fold((k * jnp.exp(gl - g)).T), kp, glast, gcol)
    qd = _each(lambda q, g: q * jnp.where(first, g[:cs], g[cs:]), qs, egc)
    egl = [jnp.where(first[:8], jnp.exp(g[cs - 1:cs, :]), jnp.exp(g[n - 1:n, :])) for g in gcol]
    for (c, p), s, a, b, d, g in zip(chains, sol, qk, qd, kdt, egl):
        uv_ref[rows[c], psl(p)] = _pair_fold(s[:, :PW])
        w_ref[rows[c], psl(p)] = _pair_fold(s[:, PW:])
        qk_ref[rows[c], psl(p)] = a
        qd_ref[rows[c], psl(p)] = b
        kdt_ref[rows[c], psl(p)] = d
        egl_ref[c, :, psl(p)] = g


def _gdn_local(q, k, v, gb):
    b, l, _ = q.shape
    r = GDN_CPG * CHUNK
    nc = l // CHUNK
    return pl.pallas_call(
        _gdn_local_kernel, grid=(b, l // r),
        in_specs=[_tok_spec(r, W)] * 3 + [_tok_spec(r, 128)],
        out_specs=[_tok_spec(r, W)] * 5 + [pl.BlockSpec((None, GDN_CPG, 8, W), lambda b, i: (b, i, 0, 0))],
        out_shape=[jax.ShapeDtypeStruct((b, l, W), F32)] * 5 + [jax.ShapeDtypeStruct((b, nc, 8, W), F32)],
        compiler_params=_cp("arbitrary", "arbitrary"), name="gdn_local",
    )(q, k, v, gb)


def _gdn_scan_kernel(uv_ref, w_ref, qk_ref, qd_ref, kdt_ref, egl_ref, z_ref, nw_ref, ones_ref, y_ref, s_ref, s_acc):
    i = pl.program_id(0)
    nb = uv_ref.shape[0]

    @pl.when(i == 0)
    def _():
        s_acc[...] = jnp.zeros(s_acc.shape, F32)

    blk = (lax.broadcasted_iota(jnp.int32, (PW, PW), 0) // DH) == (lax.broadcasted_iota(jnp.int32, (PW, PW), 1) // DH)
    chains = [(b, p) for b in range(nb) for p in range(NPAIR)]
    psl = lambda p: slice(p * PW, (p + 1) * PW)
    load = lambda ref: [ref[b, :, psl(p)] for b, p in chains]
    s = [s_acc[b, p] for b, p in chains]
    u = _each(lambda a, w, st: a - _mm3(w, st), load(uv_ref), load(w_ref), s)
    ubd = [_pair_diag(x, blk) for x in u]
    dot16 = lambda a, b: _dot(a.astype(BF16), b.astype(BF16))
    o = _each(dot16, load(qd_ref), s)
    o = _each(lambda a, qk, ut: a + dot16(qk, ut), o, load(qk_ref), ubd)
    ks = _each(lambda k, ut: _mm3(_pair_diag(k, blk), ut), load(kdt_ref), ubd)
    for (b, p), st, k in zip(chains, s, ks):
        s_acc[b, p] = egl_ref[b, 0, 0:1, psl(p)] * st + k
    zs = load(z_ref)
    for (b, p), a, z in zip(chains, o, zs):
        ms = _dot_exact_rhs(a * a, ones_ref[...]) * (1.0 / DH)
        y_ref[b, :, psl(p)] = a * lax.rsqrt(ms + RMS_EPS) * nw_ref[...] * _silu(z)

    @pl.when(i == pl.num_programs(0) - 1)
    def _():
        for b, p in chains:
            st = s_acc[b, p]
            for a in range(2):
                s_ref[b, 2 * p + a] = st[a * DH:(a + 1) * DH, a * DH:(a + 1) * DH]


def _gdn_scan(uv, w, qk, qd, kdt, egl, z, nw2, ones_pair):
    b, l, _ = uv.shape
    cs = CHUNK
    blk = pl.BlockSpec((b, cs, W), lambda i: (0, i, 0))
    return pl.pallas_call(
        _gdn_scan_kernel, grid=(l // cs,),
        in_specs=[blk] * 5 + [pl.BlockSpec((b, 1, 8, W), lambda i: (0, i, 0, 0)), blk,
                              pl.BlockSpec((1, PW), lambda i: (0, 0)), pl.BlockSpec((PW, PW), lambda i: (0, 0))],
        out_specs=[blk, pl.BlockSpec((b, NH, DH, DH), lambda i: (0, 0, 0, 0))],
        out_shape=[jax.ShapeDtypeStruct((b, l, W), F32), jax.ShapeDtypeStruct((b, NH, DH, DH), F32)],
        scratch_shapes=[pltpu.VMEM((b, NPAIR, PW, PW), F32)],
        compiler_params=_cp("arbitrary"), name="gdn_scan",
    )(uv, w, qk, qd, kdt, egl, z, nw2, ones_pair)


SB_TQ = 256
SB_TK = 128


def _sb_block(z, r, u_ref, mask):
    sp = _softplus(z)
    ln = -sp
    if mask is not None:
        ln = jnp.where(mask, ln, 0.0)
    e = _dot_exact_rhs(ln, u_ref[...])
    w = jnp.exp((z - sp) + e + r)
    if mask is not None:
        w = jnp.where(mask, w, 0.0)
    return w, r + jnp.sum(ln, axis=-1, keepdims=True)


def _sb_prompt_kernel(q_ref, k_ref, vt_ref, a_ref, o_ref, qbd_ref, nz0, nz1, w0, w1, acc_ref):
    i = pl.program_id(1)
    tq, tk = SB_TQ, SB_TK
    nzb, wb = (nz0, nz1), (w0, w1)
    tiles = [slice(h * tq, (h + 1) * tq) for h in range(NH)]
    qt = (q_ref[...] * (-(DH ** -0.5) * LOG2E)).T
    rowh = lax.broadcasted_iota(jnp.int32, (W, tq), 0) // DH
    for h, t in enumerate(tiles):
        qbd_ref[:, t] = jnp.where(rowh == h, qt, 0.0).astype(BF16)
    kk = lax.broadcasted_iota(jnp.int32, (tk, tq), 0)
    qq = lax.broadcasted_iota(jnp.int32, (tk, tq), 1)

    def stage_z(j, dst):
        kb = k_ref[pl.ds(pl.multiple_of(jnp.maximum(j, 0) * tk, tk), tk), :]
        for t in tiles:
            dst[:, t] = _dot(kb, qbd_ref[:, t])

    def stage_x(src, dst, r, mask=None):
        tot = []
        for t in tiles:
            nz = src[:, t]
            ln = jnp.minimum(nz, 0.0) - jnp.log2(1.0 + jnp.exp2(-jnp.abs(nz)))
            if mask is not None:
                ln = jnp.where(mask, ln, 0.0)
            et = _dot(a_ref[...], ln.astype(BF16))
            w = jnp.exp2((et - nz) + r[:, t])
            if mask is not None:
                w = jnp.where(mask, w, 0.0)
            dst[:, t] = w.astype(BF16)
            tot.append(et[0:1, :])
        return r + jnp.concatenate(tot, axis=1)

    def stage_p(j, src):
        vt = vt_ref[:, pl.ds(pl.multiple_of(j * tk, tk), tk)]
        for h, t in enumerate(tiles):
            acc_ref[h * DH:(h + 1) * DH, :] += _dot(vt[h * DH:(h + 1) * DH, :], src[:, t])

    assert tq == 2 * tk
    acc_ref[...] = jnp.zeros(acc_ref.shape, F32)
    r = jnp.zeros((1, NH * tq), F32)
    jb = (2 * i + 1, 2 * i)
    causal = [(j * tk + kk) < (i * tq + qq) for j in jb]
    for s in range(2):
        stage_z(jb[s], nzb[s])
    for s in range(2):
        r = stage_x(nzb[s], wb[s], r, causal[s])
    for s in range(2):
        stage_p(jb[s], wb[s])
    j0 = 2 * i - 1
    stage_z(j0, nzb[1])
    stage_z(j0 - 1, nzb[0])
    r = stage_x(nzb[1], wb[1], r)

    def tick(j, p, r):
        stage_p(j, wb[p])
        stage_z(j - 2, nzb[p])
        return stage_x(nzb[1 - p], wb[1 - p], r)

    def body(t, r):
        j = j0 - 2 * t
        return tick(j - 1, 0, tick(j, 1, r))

    lax.fori_loop(0, i, body, r)
    o_ref[...] = acc_ref[...].T


def _sb_prompt(q, k16, vt16, a_incl):
    b, l, _ = q.shape
    wide = (SB_TK, NH * SB_TQ)
    return pl.pallas_call(
        _sb_prompt_kernel, grid=(b, l // SB_TQ),
        in_specs=[_tok_spec(SB_TQ, W),
                  pl.BlockSpec((None, l, W), lambda b, i: (b, 0, 0)),
                  pl.BlockSpec((None, W, l), lambda b, i: (b, 0, 0)),
                  _full_spec((SB_TK, SB_TK))],
        out_specs=_tok_spec(SB_TQ, W),
        out_shape=jax.ShapeDtypeStruct((b, l, W), F32),
        scratch_shapes=[pltpu.VMEM((W, NH * SB_TQ), BF16)] + [pltpu.VMEM(wide, F32)] * 2 +
                       [pltpu.VMEM(wide, BF16)] * 2 + [pltpu.VMEM((W, SB_TQ), F32)],
        compiler_params=_cp("arbitrary", "arbitrary"), name="sb_prompt",
    )(q, k16, vt16, a_incl)


def _strict_upper(n):
    i = jnp.arange(n)
    return (i[:, None] > i[None, :]).astype(BF16)


def _convprep_sample_kernel(c3_ref, qkv_ref, ba_ref, cb0_ref, cd0_ref, wb_ref, wd_ref, gp_ref, ones_ref,
                            yb_ref, qt_ref, kt_ref, vt_ref, gbt_ref, cbs_ref, cds_ref):
    nt = c3_ref.shape[0]
    xb = [cb0_ref[0], cb0_ref[1]]
    xd = [cd0_ref[0], cd0_ref[1], cd0_ref[2]]
    for t in range(nt):
        c3 = c3_ref[t]
        xb.append(c3[:, W:2 * W] * c3[:, 2 * W:])
        xd.append(qkv_ref[t])
    for t in range(nt):
        conv = xb[t] * wb_ref[0:1, :] + xb[t + 1] * wb_ref[1:2, :] + xb[t + 2] * wb_ref[2:3, :]
        yb_ref[t] = c3_ref[t][:, :W] * conv
        conv = (xd[t] * wd_ref[0:1, :] + xd[t + 1] * wd_ref[1:2, :] + xd[t + 2] * wd_ref[2:3, :]
                + xd[t + 3] * wd_ref[3:4, :])
        q, k, v, gb = _delta_act(conv, ba_ref[t], gp_ref, ones_ref)
        qt_ref[t] = q.T
        kt_ref[t] = k.T
        vt_ref[t] = v.T
        gbt_ref[t] = gb.T[0:8, :]
    cbs_ref[0] = xb[nt]
    cbs_ref[1] = xb[nt + 1]
    for j in range(3):
        cds_ref[j] = xd[nt + j]


def _convprep_sample(c3, qkv, ba, cb0, cd0, wb, wd, gp, ones_bd):
    t, s, _ = c3.shape
    sds = jax.ShapeDtypeStruct
    return pl.pallas_call(
        _convprep_sample_kernel,
        out_shape=[sds((t, s, W), F32), sds((t, W, s), F32), sds((t, W, s), F32), sds((t, W, s), F32),
                   sds((t, 8, s), F32), sds((2, s, W), F32), sds((3, s, 3 * W), F32)],
        compiler_params=pltpu.CompilerParams(vmem_limit_bytes=VMEM_LIMIT), name="convprep_sample",
    )(c3, qkv, ba, cb0, cd0, wb, wd, gp, ones_bd)


def _gdn_sample_kernel(qt_ref, kt_ref, vt_ref, gbt_ref, s0_ref, ot_ref, s1_ref):
    h = pl.program_id(0)
    nt = qt_ref.shape[0]
    ns = qt_ref.shape[2]
    for t in range(nt):
        src = s0_ref if t == 0 else s1_ref
        b = gbt_ref[t, pl.ds(h, 1), :]
        a = jnp.exp(gbt_ref[t, pl.ds(NH + h, 1), :])

        def ks_body(i, acc):
            return acc + kt_ref[t, pl.ds(i, 1), :] * src[i]

        ks = lax.fori_loop(0, DH, ks_body, jnp.zeros((DH, ns), F32), unroll=8)
        u = b * (vt_ref[t] - a * ks)

        def up_body(i, acc):
            sn = a * src[i] + kt_ref[t, pl.ds(i, 1), :] * u
            s1_ref[i] = sn
            return acc + qt_ref[t, pl.ds(i, 1), :] * sn

        ot_ref[t] = lax.fori_loop(0, DH, up_body, jnp.zeros((DH, ns), F32), unroll=8)


def _gdn_sample(qt, kt, vt, gbt, s0):
    t, _, s = qt.shape
    hb = pl.BlockSpec((t, DH, s), lambda h: (0, h, 0))
    sb = pl.BlockSpec((None, DH, DH, s), lambda h: (h, 0, 0, 0))
    return pl.pallas_call(
        _gdn_sample_kernel, grid=(NH,),
        in_specs=[hb, hb, hb, pl.BlockSpec((t, 8, s), lambda h: (0, 0, 0)), sb],
        out_specs=[hb, sb],
        out_shape=[jax.ShapeDtypeStruct((t, W, s), F32), jax.ShapeDtypeStruct((NH, DH, DH, s), F32)],
        compiler_params=_cp("arbitrary"), name="gdn_sample",
    )(qt, kt, vt, gbt, s0)


def _gdn_post_sample_kernel(ot_ref, z_ref, nw_ref, ones_ref, y_ref):
    for t in range(ot_ref.shape[0]):
        o = ot_ref[t].T
        ms = _head_sumsq(o, ones_ref) * (1.0 / DH)
        y_ref[t] = o * lax.rsqrt(ms + RMS_EPS) * nw_ref[...] * _silu(z_ref[t])


def _gdn_post_sample(ot, z, nw4, ones_bd):
    t, _, s = ot.shape
    return pl.pallas_call(
        _gdn_post_sample_kernel, out_shape=jax.ShapeDtypeStruct((t, s, W), F32), name="gdn_post_sample",
    )(ot, z, nw4, ones_bd)


def _sb_sample_kernel(pt_ref, q_ref, kn_ref, vn_ref, *rest):
    n_pages = (len(rest) - 4) // 2
    kp = rest[:n_pages]
    vp = rest[n_pages:2 * n_pages]
    u_ref, o_ref, kpad, vpad = rest[2 * n_pages:]
    nt = q_ref.shape[0]
    nq = NH * nt
    row = lax.broadcasted_iota(jnp.int32, (nq, W), 0)
    lane = lax.broadcasted_iota(jnp.int32, (nq, W), 1)
    q = q_ref[...] * (DH ** -0.5)
    qbd = jnp.where(row // nt == lane // DH, jnp.concatenate([q] * NH, axis=0), 0.0).astype(BF16)

    kpad[...] = jnp.zeros(kpad.shape, F32)
    vpad[...] = jnp.zeros(vpad.shape, F32)
    kpad[0:nt, :] = kn_ref[...]
    vpad[0:nt, :] = vn_ref[...]
    qi = lax.broadcasted_iota(jnp.int32, (nq, PAGE), 0) % nt
    ki = lax.broadcasted_iota(jnp.int32, (nq, PAGE), 1)
    z = _dot_nt(qbd, kpad[...].astype(BF16))
    w, r = _sb_block(z, jnp.zeros((nq, 1), F32), u_ref, ki < qi)
    acc = _dot(w.astype(BF16), vpad[...].astype(BF16))

    pages = list(range(n_pages - 1, -1, -1))
    z = [_dot(qbd, kp[p][...].astype(BF16)) for p in pages]
    sp = _each(_softplus, z)
    e = _each(lambda s: _dot_exact_rhs(-s, u_ref[...]), sp)
    for p, zp, spp, ep in zip(pages, z, sp, e):
        w = jnp.exp((zp - spp) + ep + r)
        acc = acc + _dot_nt(w.astype(BF16), vp[p][...].astype(BF16))
        r = r - jnp.sum(spp, axis=-1, keepdims=True)

    sel = jnp.where(row // nt == lane // DH, acc, 0.0)
    out = sel[0:nt]
    for h in range(1, NH):
        out = out + sel[h * nt:(h + 1) * nt]
    o_ref[...] = out


def _sb_sample(page_table, q, kn, vn, cache_kt, cache_vt, layer, u_strict):
    s, t, _ = q.shape
    n_pages = page_table.shape[1]
    seq = pl.BlockSpec((None, t, W), lambda s_, pt: (s_, 0, 0))

    def page_spec(p):
        return pl.BlockSpec((None, None, W, PAGE), lambda s_, pt: (layer, pt[s_, p], 0, 0))

    grid_spec = pltpu.PrefetchScalarGridSpec(
        num_scalar_prefetch=1, grid=(s,),
        in_specs=[seq, seq, seq] + [page_spec(p) for p in range(n_pages)] * 2 +
                 [pl.BlockSpec((PAGE, PAGE), lambda s_, pt: (0, 0))],
        out_specs=seq,
        scratch_shapes=[pltpu.VMEM((PAGE, W), F32), pltpu.VMEM((PAGE, W), F32)])
    return pl.pallas_call(
        _sb_sample_kernel, grid_spec=grid_spec,
        out_shape=jax.ShapeDtypeStruct((s, t, W), F32),
        compiler_params=_cp("arbitrary"), name="sb_sample",
    )(page_table, q, kn, vn, *([cache_kt] * n_pages), *([cache_vt] * n_pages), u_strict)


def _layer_weights(l, w_in, ssm_a_re, ssm_a_im, ssm_log_dt, ssm_b_re, ssm_b_im, ssm_c_re, ssm_c_im, ssm_d, w_glu,
                   conv_b_w, delta_conv_w, delta_a_log, delta_dt_bias, delta_norm_w, w_branch, w_gate, w_o,
                   ln1_g, ln1_b, w_ffn_up, w_ffn_down, ln2_g, ln2_b):
    w = w_in[l]
    w_in_p = jnp.concatenate([w[:, :2048], jnp.pad(w[:, 2048:2056], ((0, 0), (0, 120))), w[:, 2056:]],
                             axis=1).astype(BF16)
    apow, bbd = _s5_prep(ssm_a_re[l], ssm_a_im[l], ssm_log_dt[l], ssm_b_re[l], ssm_b_im[l])
    gp = jnp.zeros((2, 128), F32)
    gp = gp.at[0, NH:2 * NH].set(delta_a_log[l]).at[1, NH:2 * NH].set(delta_dt_bias[l])
    hid = jnp.arange(W) // DH
    return dict(
        w_in=w_in_p, apow=apow, bbd=bbd.astype(BF16), cre=_s5_cmat(ssm_c_re[l]).astype(BF16),
        cim=_s5_cmat(ssm_c_im[l]).astype(BF16), d=ssm_d[l].reshape(1, W), wglu=w_glu[l].astype(BF16),
        wb=conv_b_w[l], wd=delta_conv_w[l], gp=gp,
        nw4=jnp.tile(delta_norm_w[l], NH).reshape(1, W),
        ones_bd=(hid[:, None] == hid[None, :]).astype(BF16),
        wbr=w_branch[l].astype(BF16), wg=w_gate[l].astype(BF16), wo=w_o[l].astype(BF16),
        ln1g=ln1_g[l].reshape(1, D), ln1b=ln1_b[l].reshape(1, D),
        wu=w_ffn_up[l].astype(BF16), wdn=w_ffn_down[l].astype(BF16),
        ln2g=ln2_g[l].reshape(1, D), ln2b=ln2_b[l].reshape(1, D))


def _prompt_layer(x, mod, p, u_strict):
    b, l, _ = x.shape
    u, c3, qkv, z, ba, q, kt, vt, k16, vt16 = _proj(x, mod, p["w_in"], 512, True)
    ya, hre, him = _s5_prompt(u, p["apow"], p["bbd"], p["cre"], p["cim"], p["d"], p["wglu"])
    yb, qd, kd, vd, gb, cbs, cds = _convprep_prompt(c3, qkv, ba, p["wb"], p["wd"], p["gp"], p["ones_bd"])
    uv, w, qk, qdec, kdt, egl = _gdn_local(qd, kd, vd, gb)
    yc, s_new = _gdn_scan(uv, w, qk, qdec, kdt, egl, z, p["nw4"][:, :PW], p["ones_bd"][:PW, :PW])
    idx = jnp.arange(SB_TK)
    a_incl = (idx[None, :] >= idx[:, None]).astype(BF16)
    yd = _sb_prompt(q, k16, vt16, a_incl)
    x1 = _merge(x, mod, ya, yb, yc, yd, p["wg"], p["wbr"], p["wo"], p["ln1g"], p["ln1b"], 256)
    x2 = _ffn(x1, mod, p["wu"], p["wdn"], p["ln2g"], p["ln2b"], 512)
    unt = lambda t: t.reshape(b, NH, DH, l).transpose(0, 3, 1, 2)
    new = (unt(kt), unt(vt), hre.reshape(b, G, P), him.reshape(b, G, P),
           cbs[:, 6:8], s_new, cds[:, 5:8])
    return x2, new


def _sample_layer(x, mod, p, st, page_table, cache_k, cache_v, layer, u_strict):
    h0re, h0im, cb0, s0, cd0 = st
    s = h0re.shape[0]
    n = x.shape[1]
    t = n // s
    u, c3, qkv, z, ba, q, k, v = _proj(x, mod, p["w_in"], n, False)
    tm = lambda a: a.reshape(t, s, a.shape[-1])
    sm = lambda a: tm(a).transpose(1, 0, 2)
    ya, hre, him = _s5_sample(tm(u), h0re.reshape(s, NS), h0im.reshape(s, NS), p["apow"], p["bbd"], p["cre"],
                              p["cim"], p["d"], p["wglu"])
    yb, qt, kt, vt, gbt, cbs, cds = _convprep_sample(tm(c3), tm(qkv), tm(ba), cb0.transpose(1, 0, 2),
                                                      cd0.transpose(1, 0, 2), p["wb"], p["wd"], p["gp"],
                                                      p["ones_bd"])
    ot, s1 = _gdn_sample(qt, kt, vt, gbt, s0.transpose(1, 2, 3, 0))
    yc = _gdn_post_sample(ot, tm(z), p["nw4"], p["ones_bd"])
    ksm, vsm = sm(k), sm(v)
    yd = _sb_sample(page_table, sm(q), ksm, vsm, cache_k, cache_v, layer, u_strict).transpose(1, 0, 2)
    flat = lambda a: a.reshape(1, n, W)
    x1 = _merge(x, mod, flat(ya), flat(yb), flat(yc), flat(yd), p["wg"], p["wbr"], p["wo"], p["ln1g"], p["ln1b"],
                min(256, n))
    x2 = _ffn(x1, mod, p["wu"], p["wdn"], p["ln2g"], p["ln2b"], n)
    new = (ksm.reshape(s, t, NH, DH), vsm.reshape(s, t, NH, DH), hre.reshape(s, G, P), him.reshape(s, G, P),
           cbs.transpose(1, 0, 2), s1.transpose(3, 0, 1, 2), cds.transpose(1, 0, 2))
    return x2, new


def kernel(x_prompt, x_sample, cache_k, cache_v, state_ssm_re, state_ssm_im, state_conv_b, state_delta,
           state_conv_delta, page_table, c_prompt, c_sample, w_ada, b_ada, w_in, ssm_a_re, ssm_a_im, ssm_log_dt,
           ssm_b_re, ssm_b_im, ssm_c_re, ssm_c_im, ssm_d, w_glu, conv_b_w, delta_conv_w, delta_a_log,
           delta_dt_bias, delta_norm_w, w_branch, w_gate, w_o, ln1_g, ln1_b, w_ffn_up, w_ffn_down, ln2_g, ln2_b):
    depth = w_ada.shape[0]
    bp, lp, _ = x_prompt.shape
    bs, ts, _ = x_sample.shape
    n_pool = cache_k.shape[1]

    c_all = jnp.concatenate([c_prompt, c_sample], axis=0)
    pad = (-c_all.shape[0]) % 8
    mod = _mod(jnp.pad(c_all, ((0, pad), (0, 0))), w_ada, b_ada)
    mod_p = mod[:, :bp].reshape(depth, bp, 6, 1, D)
    mod_s = mod[:, bp:bp + bs].reshape(depth, bs, 6, D).transpose(0, 2, 1, 3)
    mod_s = jnp.broadcast_to(mod_s[:, :, None], (depth, 6, ts, bs, D)).reshape(depth, 1, 6, ts * bs, D)

    ck = cache_k.transpose(0, 1, 3, 4, 2).reshape(depth, n_pool, W, PAGE)
    cv = cache_v.transpose(0, 1, 3, 4, 2).reshape(depth, n_pool, W, PAGE)
    u_strict = _strict_upper(PAGE)

    x_p = x_prompt
    x_s = x_sample.transpose(1, 0, 2).reshape(1, ts * bs, D)
    new_p, new_s = [], []
    for l in range(depth):
        p = _layer_weights(l, w_in, ssm_a_re, ssm_a_im, ssm_log_dt, ssm_b_re, ssm_b_im, ssm_c_re, ssm_c_im, ssm_d,
                           w_glu, conv_b_w, delta_conv_w, delta_a_log, delta_dt_bias, delta_norm_w, w_branch,
                           w_gate, w_o, ln1_g, ln1_b, w_ffn_up, w_ffn_down, ln2_g, ln2_b)
        x_p, np_ = _prompt_layer(x_p, mod_p[l], p, u_strict)
        new_p.append(np_)
        st = (state_ssm_re[l], state_ssm_im[l], state_conv_b[l], state_delta[l], state_conv_delta[l])
        x_s, ns_ = _sample_layer(x_s, mod_s[l], p, st, page_table, ck, cv, l, u_strict)
        new_s.append(ns_)
    k_p, v_p, re_p, im_p, cb_p, d_p, cd_p = [jnp.stack(t, axis=0) for t in zip(*new_p)]
    k_s, v_s, re_s, im_s, cb_s, d_s, cd_s = [jnp.stack(t, axis=0) for t in zip(*new_s)]
    y_s = x_s.reshape(ts, bs, D).transpose(1, 0, 2)
    return (x_p, y_s, k_p, v_p, k_s, v_s, re_p, im_p, re_s, im_s, cb_p, cb_s, d_p, d_s, cd_p, cd_s)
```
